```python
import math
import jax, jax.numpy as jnp
from jax import lax
import numpy as np

D_MODEL = 1024
BATCH = 4
SEQ = 8192
DEPTH = 2

HEAD_DIM = 64
N_HEADS_A = 8
N_HEADS_B = 8
MOBA_BLOCK = 256
MOBA_TOPK = 3
MOBA_QCHUNK = 32
DILATED_PAIRS = ((128, 1), (512, 4), (2048, 16))
BAND_BLOCK = 128
ROPE_THETA = 10000.0

GDN_QK_HEADS = 8
GDN_V_HEADS = 16
GDN_HEAD_DIM = 128
GDN_CONV = 4
GDN_CHUNK = 64

N_EXPERTS = 32
N_GROUPS = 4
EXPERTS_PER_GROUP = N_EXPERTS // N_GROUPS
TOP_K = 2
D_FF_EXPERT = 512
MOE_BLOCK = 256

LN_EPS = 1e-5
NORM_EPS = 1e-6
NEG = -1e30

N_EVEN = (DEPTH + 1) // 2
N_ODD = DEPTH // 2
AB_HEADS = N_HEADS_A + N_HEADS_B
AB_IN = 3 * AB_HEADS * HEAD_DIM
AB_OUT = AB_HEADS * HEAD_DIM
GDN_QK = GDN_QK_HEADS * GDN_HEAD_DIM
GDN_V = GDN_V_HEADS * GDN_HEAD_DIM
GDN_CONV_CH = 2 * GDN_QK + GDN_V
GDN_IN = GDN_CONV_CH + GDN_V + 2 * GDN_V_HEADS
DEEPNORM_ALPHA = (2.0 * DEPTH) ** 0.25
DEEPNORM_BETA = (8.0 * DEPTH) ** -0.25

kernel_name = "moba_dilated_gdn_grouped_moe_deepnorm"

f32 = jnp.float32


def layer_norm(x, g, b):
    xf = x.astype(f32)
    mu = xf.mean(-1, keepdims=True)
    var = jnp.square(xf - mu).mean(-1, keepdims=True)
    return ((xf - mu) * lax.rsqrt(var + LN_EPS) * g.astype(f32) + b.astype(f32)).astype(x.dtype)


def rope_tables(seq_len, dim):
    inv = ROPE_THETA ** (-jnp.arange(0, dim, 2, dtype=f32) / dim)
    ang = jnp.arange(seq_len, dtype=f32)[:, None] * inv[None, :]
    return jnp.cos(ang), jnp.sin(ang)


def apply_rope(x, cos, sin):
    x1, x2 = jnp.split(x.astype(f32), 2, axis=-1)
    return jnp.concatenate([x1 * cos - x2 * sin, x2 * cos + x1 * sin], -1).astype(x.dtype)


def moba_attention(q, k, v):
    bsz, nh, seq, dh = q.shape
    nblk = -(-seq // MOBA_BLOCK)
    s_pad = nblk * MOBA_BLOCK
    pad = ((0, 0), (0, 0), (0, s_pad - seq), (0, 0))
    q, k, v = jnp.pad(q, pad), jnp.pad(k, pad), jnp.pad(v, pad)
    scale = dh ** -0.5
    kblk = k.reshape(bsz, nh, nblk, MOBA_BLOCK, dh)
    vblk = v.reshape(bsz, nh, nblk, MOBA_BLOCK, dh)
    kmean = kblk.astype(f32).mean(3)
    qblk = jnp.arange(s_pad) // MOBA_BLOCK
    gate = jnp.einsum('bhsd,bhnd->bhsn', q.astype(f32), kmean)
    fully_past = jnp.arange(nblk)[None, :] < qblk[:, None]
    gate = jnp.where(fully_past, gate, NEG)
    ksel = min(MOBA_TOPK, nblk)
    _, sel = lax.top_k(gate, ksel)
    valid = sel < qblk[:, None]
    nchunk = s_pad // MOBA_QCHUNK
    bi = jnp.arange(bsz)[:, None, None]
    hi = jnp.arange(nh)[None, :, None]
    n_sel = ksel * MOBA_BLOCK

    def to_chunks(t):
        return jnp.moveaxis(t.reshape(bsz, nh, nchunk, MOBA_QCHUNK, t.shape[-1]), 2, 0)

    def step(args):
        qc, selc, validc, c = args
        q0 = c * MOBA_QCHUNK
        ob = q0 // MOBA_BLOCK
        flat = selc.reshape(bsz, nh, MOBA_QCHUNK * ksel)
        kg = kblk[bi, hi, flat].reshape(bsz, nh, MOBA_QCHUNK, n_sel, dh)
        vg = vblk[bi, hi, flat].reshape(bsz, nh, MOBA_QCHUNK, n_sel, dh)
        s_sel = jnp.einsum('bhqd,bhqkd->bhqk', qc, kg).astype(f32) * scale
        s_sel = jnp.where(jnp.repeat(validc, MOBA_BLOCK, axis=-1), s_sel, NEG)
        kown = lax.dynamic_index_in_dim(kblk, ob, axis=2, keepdims=False)
        vown = lax.dynamic_index_in_dim(vblk, ob, axis=2, keepdims=False)
        s_own = jnp.einsum('bhqd,bhkd->bhqk', qc, kown).astype(f32) * scale
        causal = (ob * MOBA_BLOCK + jnp.arange(MOBA_BLOCK))[None, :] <= (q0 + jnp.arange(MOBA_QCHUNK))[:, None]
        s_own = jnp.where(causal, s_own, NEG)
        p = jax.nn.softmax(jnp.concatenate([s_sel, s_own], -1), axis=-1).astype(v.dtype)
        return (jnp.einsum('bhqk,bhqkd->bhqd', p[..., :n_sel], vg)
                + jnp.einsum('bhqk,bhkd->bhqd', p[..., n_sel:], vown))

    out = lax.map(step, (to_chunks(q), to_chunks(sel), to_chunks(valid), jnp.arange(nchunk)))
    out = jnp.moveaxis(out, 0, 2).reshape(bsz, nh, s_pad, dh)
    return out[:, :, :seq]


def band_attention(q, k, v, reach):
    *lead, seq_len, dh = q.shape
    w = BAND_BLOCK
    n = seq_len // w
    qb = q.reshape(*lead, n, w, dh)
    kb = k.reshape(*lead, n, w, dh)
    vb = v.reshape(*lead, n, w, dh)

    def with_prev(t):
        prev = jnp.concatenate([jnp.zeros_like(t[..., :1, :, :]), t[..., :-1, :, :]], axis=-3)
        return jnp.concatenate([prev, t], axis=-2)

    kc, vc = with_prev(kb), with_prev(vb)
    s = jnp.einsum('...nqd,...nkd->...nqk', qb, kc).astype(f32) * dh ** -0.5
    dist = (w + jnp.arange(w))[:, None] - jnp.arange(2 * w)[None, :]
    allowed = (dist >= 0) & (dist <= reach)
    first = (jnp.arange(n) == 0)[:, None, None] & (jnp.arange(2 * w) < w)[None, None, :]
    s = jnp.where(allowed[None] & ~first, s, NEG)
    m = s.max(-1, keepdims=True)
    e = jnp.exp(s - m)
    l = e.sum(-1, keepdims=True)
    o = jnp.einsum('...nqk,...nkd->...nqd', (e / l).astype(v.dtype), vc)
    lse = (m + jnp.log(l))[..., 0]
    return o.reshape(*lead, seq_len, dh), lse.reshape(*lead, seq_len)


def to_strided(t, dil, padded_len):
    bsz, nh, seq, dh = t.shape
    t = jnp.swapaxes(t.reshape(bsz, nh, seq // dil, dil, dh), 2, 3)
    return jnp.pad(t, ((0, 0), (0, 0), (0, 0), (0, padded_len - seq // dil), (0, 0)))


def dilated_attention(q, k, v):
    bsz, nh, seq, dh = q.shape
    outs, lses = [], []
    for window, dil in DILATED_PAIRS:
        sub_len = seq // dil
        padded_len = -(-sub_len // BAND_BLOCK) * BAND_BLOCK
        o, lse = band_attention(to_strided(q, dil, padded_len), to_strided(k, dil, padded_len),
                                to_strided(v, dil, padded_len), window // dil)
        outs.append(jnp.swapaxes(o[..., :sub_len, :], 2, 3).reshape(bsz, nh, seq, dh))
        lses.append(jnp.swapaxes(lse[..., :sub_len], 2, 3).reshape(bsz, nh, seq))
    wts = jax.nn.softmax(jnp.stack(lses), axis=0)
    return jnp.einsum('gbhs,gbhsd->bhsd', wts.astype(q.dtype), jnp.stack(outs))


def attention_mixer(x, w_in, w_out, cos, sin):
    bsz, seq, _ = x.shape
    h = (x @ w_in).reshape(bsz, seq, 3, AB_HEADS, HEAD_DIM)
    h = jnp.moveaxis(h, 1, 3)
    q = apply_rope(h[:, 0], cos, sin)
    k = apply_rope(h[:, 1], cos, sin)
    v = h[:, 2]
    o_a = moba_attention(q[:, :N_HEADS_A], k[:, :N_HEADS_A], v[:, :N_HEADS_A])
    o_b = dilated_attention(q[:, N_HEADS_A:], k[:, N_HEADS_A:], v[:, N_HEADS_A:])
    o = jnp.concatenate([o_a, o_b], axis=1)
    return jnp.swapaxes(o, 1, 2).reshape(bsz, seq, AB_OUT) @ w_out


def causal_dwconv(x, w):
    width = w.shape[0]
    return lax.conv_general_dilated(x, w[:, None, :].astype(x.dtype), window_strides=(1,),
                                    padding=[(width - 1, 0)], dimension_numbers=('NWC', 'WIO', 'NWC'),
                                    feature_group_count=x.shape[-1])


def l2norm(t):
    return t * lax.rsqrt(jnp.sum(t * t, -1, keepdims=True) + NORM_EPS)


def gated_delta_rule(q, k, v, g, beta):
    bsz, nh, seq, dk = q.shape
    dv = v.shape[-1]
    c = GDN_CHUNK
    n = seq // c
    q = q * dk ** -0.5
    q, k, v = (t.reshape(bsz, nh, n, c, t.shape[-1]) for t in (q, k, v))
    beta = beta.reshape(bsz, nh, n, c)[..., None]
    gc = jnp.cumsum(g.reshape(bsz, nh, n, c), axis=-1)
    ii = jnp.arange(c)
    lower = ii[:, None] >= ii[None, :]
    strict = ii[:, None] > ii[None, :]
    diff = gc[..., :, None] - gc[..., None, :]
    decay = jnp.where(lower, jnp.exp(jnp.where(lower, diff, 0.0)), 0.0)
    kb = k * beta
    l_mat = jnp.where(strict, jnp.einsum('...ik,...jk->...ij', kb, k) * decay, 0.0)
    a_mat = l_mat + jnp.eye(c, dtype=f32)
    rhs = jnp.concatenate([v * beta, kb * jnp.exp(gc)[..., None]], -1)
    sol = lax.linalg.triangular_solve(a_mat, rhs, left_side=True, lower=True, unit_diagonal=True)
    u, w = sol[..., :dv], sol[..., dv:]
    attn = jnp.einsum('...ik,...jk->...ij', q, k) * decay
    q_dec = q * jnp.exp(gc)[..., None]
    k_dec = k * jnp.exp(gc[..., -1:] - gc)[..., None]
    chunk_decay = jnp.exp(gc[..., -1])

    def step(state, xs):
        u_i, w_i, q_i, k_i, a_i, d_i = xs
        v_new = u_i - jnp.einsum('bhck,bhkv->bhcv', w_i, state)
        o_i = jnp.einsum('bhck,bhkv->bhcv', q_i, state) + jnp.einsum('bhcj,bhjv->bhcv', a_i, v_new)
        state = state * d_i[..., None, None] + jnp.einsum('bhck,bhcv->bhkv', k_i, v_new)
        return state, o_i

    xs = tuple(jnp.moveaxis(t, 2, 0) for t in (u, w, q_dec, k_dec, attn, chunk_decay))
    _, o = lax.scan(step, jnp.zeros((bsz, nh, dk, dv), f32), xs)
    return jnp.moveaxis(o, 0, 2).reshape(bsz, nh, seq, dv)


def gdn_mixer(x, w_in, conv_w, a_log, dt_bias, norm_g, w_out):
    bsz, seq, _ = x.shape
    h = x @ w_in
    qkv, z, b, a = jnp.split(h, [GDN_CONV_CH, GDN_CONV_CH + GDN_V, GDN_CONV_CH + GDN_V + GDN_V_HEADS], axis=-1)
    qkv = jax.nn.silu(causal_dwconv(qkv, conv_w))
    q, k, v = jnp.split(qkv, [GDN_QK, 2 * GDN_QK], axis=-1)
    rep = GDN_V_HEADS // GDN_QK_HEADS

    def heads(t, n):
        return jnp.swapaxes(t.astype(f32).reshape(bsz, seq, n, GDN_HEAD_DIM), 1, 2)

    q = jnp.repeat(l2norm(heads(q, GDN_QK_HEADS)), rep, axis=1)
    k = jnp.repeat(l2norm(heads(k, GDN_QK_HEADS)), rep, axis=1)
    v = heads(v, GDN_V_HEADS)
    beta = jnp.swapaxes(jax.nn.sigmoid(b.astype(f32)), 1, 2)
    g = -jnp.exp(a_log.astype(f32)) * jax.nn.softplus(a.astype(f32) + dt_bias.astype(f32))
    g = jnp.swapaxes(g, 1, 2)
    o = jnp.swapaxes(gated_delta_rule(q, k, v, g, beta), 1, 2)
    zf = z.astype(f32).reshape(bsz, seq, GDN_V_HEADS, GDN_HEAD_DIM)
    o = o * lax.rsqrt(jnp.mean(o * o, -1, keepdims=True) + NORM_EPS) * norm_g.astype(f32) * jax.nn.silu(zf)
    return o.reshape(bsz, seq, GDN_V).astype(x.dtype) @ w_out


def moe_ffn(x, router_w, router_bias, w1, w3, w2):
    bsz, seq, d = x.shape
    n_tok = bsz * seq
    xf = x.reshape(n_tok, d)
    scores = jax.nn.sigmoid(xf.astype(f32) @ router_w.astype(f32))
    sel = (scores + router_bias.astype(f32)).reshape(n_tok, N_GROUPS, EXPERTS_PER_GROUP)
    group_score = lax.top_k(sel, TOP_K)[0].sum(-1)
    grp = jnp.argmax(group_score, axis=-1)
    in_grp = jnp.take_along_axis(sel, grp[:, None, None], axis=1)[:, 0]
    _, local = lax.top_k(in_grp, TOP_K)
    expert = grp[:, None] * EXPERTS_PER_GROUP + local
    gate = jnp.take_along_axis(scores, expert, axis=1)
    gate = gate / gate.sum(-1, keepdims=True)
    n_asg = n_tok * TOP_K
    e_flat = expert.reshape(-1)
    tok_flat = jnp.repeat(jnp.arange(n_tok, dtype=jnp.int32), TOP_K)
    g_flat = gate.reshape(-1)
    order = jnp.argsort(e_flat)
    e_sorted = e_flat[order]
    counts = jnp.zeros((N_EXPERTS,), jnp.int32).at[e_flat].add(1)
    padded = (counts + MOE_BLOCK - 1) // MOE_BLOCK * MOE_BLOCK
    pad_end = jnp.cumsum(padded)
    pad_start = pad_end - padded
    cnt_start = jnp.cumsum(counts) - counts
    dest = pad_start[e_sorted] + jnp.arange(n_asg) - cnt_start[e_sorted]
    n_blocks = -(-n_asg // MOE_BLOCK) + N_EXPERTS
    cap = n_blocks * MOE_BLOCK
    slot_tok = jnp.full((cap,), n_tok, jnp.int32).at[dest].set(tok_flat[order])
    slot_gate = jnp.zeros((cap,), f32).at[dest].set(g_flat[order])
    block_expert = jnp.minimum(jnp.searchsorted(pad_end, jnp.arange(n_blocks) * MOE_BLOCK, side='right'),
                               N_EXPERTS - 1)
    x_pad = jnp.concatenate([xf, jnp.zeros((1, d), xf.dtype)], axis=0)
    xs = x_pad[slot_tok].reshape(n_blocks, MOE_BLOCK, d)

    def expert_block(args):
        xb, e = args
        hb = jax.nn.silu(xb @ w1[e]) * (xb @ w3[e])
        return hb @ w2[e]

    ys = lax.map(expert_block, (xs, block_expert)).reshape(cap, d)
    out = jnp.zeros((n_tok + 1, d), f32).at[slot_tok].add(ys.astype(f32) * slot_gate[:, None])
    return out[:n_tok].reshape(bsz, seq, d).astype(x.dtype)


def setup_inputs(seed: int = 0) -> dict:
    key = jax.random.key(seed)
    ks = jax.random.split(key, 20)

    def nrm(k, shape, scale):
        return jax.random.normal(k, shape, f32) * scale

    beta = DEEPNORM_BETA
    dt = jnp.exp(jax.random.uniform(ks[6], (N_ODD, GDN_V_HEADS), f32, math.log(1e-3), math.log(1e-1)))
    return {
        "x": nrm(ks[0], (BATCH, SEQ, D_MODEL), 1.0),
        "ab_w_in": nrm(ks[1], (N_EVEN, D_MODEL, AB_IN), D_MODEL ** -0.5),
        "ab_w_out": nrm(ks[2], (N_EVEN, AB_OUT, D_MODEL), beta * AB_OUT ** -0.5),
        "gdn_w_in": nrm(ks[3], (N_ODD, D_MODEL, GDN_IN), D_MODEL ** -0.5),
        "gdn_conv_w": nrm(ks[4], (N_ODD, GDN_CONV, GDN_CONV_CH), GDN_CONV ** -0.5),
        "gdn_a_log": jnp.log(jax.random.uniform(ks[5], (N_ODD, GDN_V_HEADS), f32, 1.0, 16.0)),
        "gdn_dt_bias": dt + jnp.log(-jnp.expm1(-dt)),
        "gdn_norm_g": 1.0 + nrm(ks[7], (N_ODD, GDN_HEAD_DIM), 0.02),
        "gdn_w_out": nrm(ks[8], (N_ODD, GDN_V, D_MODEL), beta * GDN_V ** -0.5),
        "mix_ln_g": 1.0 + nrm(ks[9], (DEPTH, D_MODEL), 0.02),
        "mix_ln_b": nrm(ks[10], (DEPTH, D_MODEL), 0.02),
        "router_w": nrm(ks[11], (D_MODEL, N_EXPERTS), D_MODEL ** -0.5),
        "router_bias": nrm(ks[12], (N_EXPERTS,), 0.01),
        "moe_w1": nrm(ks[13], (DEPTH, N_EXPERTS, D_MODEL, D_FF_EXPERT), D_MODEL ** -0.5),
        "moe_w3": nrm(ks[14], (DEPTH, N_EXPERTS, D_MODEL, D_FF_EXPERT), D_MODEL ** -0.5),
        "moe_w2": nrm(ks[15], (DEPTH, N_EXPERTS, D_FF_EXPERT, D_MODEL), beta * D_FF_EXPERT ** -0.5),
        "ffn_ln_g": 1.0 + nrm(ks[16], (DEPTH, D_MODEL), 0.02),
        "ffn_ln_b": nrm(ks[17], (DEPTH, D_MODEL), 0.02),
    }


def reference(x, ab_w_in, ab_w_out, gdn_w_in, gdn_conv_w, gdn_a_log, gdn_dt_bias, gdn_norm_g, gdn_w_out,
              mix_ln_g, mix_ln_b, router_w, router_bias, moe_w1, moe_w3, moe_w2, ffn_ln_g, ffn_ln_b):
    seq = x.shape[1]
    cos, sin = rope_tables(seq, HEAD_DIM)
    for layer in range(DEPTH):
        j = layer // 2
        if layer % 2 == 0:
            mix = attention_mixer(x, ab_w_in[j], ab_w_out[j], cos, sin)
        else:
            mix = gdn_mixer(x, gdn_w_in[j], gdn_conv_w[j], gdn_a_log[j], gdn_dt_bias[j], gdn_norm_g[j], gdn_w_out[j])
        x = layer_norm(DEEPNORM_ALPHA * x + mix, mix_ln_g[layer], mix_ln_b[layer])
        ffn = moe_ffn(x, router_w, router_bias, moe_w1[layer], moe_w3[layer], moe_w2[layer])
        x = layer_norm(DEEPNORM_ALPHA * x + ffn, ffn_ln_g[layer], ffn_ln_b[layer])
    return x
```

```python
import functools

import numpy as np
import jax
import jax.numpy as jnp
from jax import lax
from jax.experimental import pallas as pl
from jax.experimental.pallas import tpu as pltpu

f32 = jnp.float32
bf16 = jnp.bfloat16
i32 = jnp.int32

HEAD_DIM = 64
N_HEADS_A = 8
N_HEADS_B = 8
AB_HEADS = N_HEADS_A + N_HEADS_B
MOBA_BLOCK = 256
MOBA_TOPK = 3
DILATED_PAIRS = ((128, 1), (512, 4), (2048, 16))
BAND_BLOCK = 128
ROPE_THETA = 10000.0

GDN_QK_HEADS = 8
GDN_V_HEADS = 16
GDN_HEAD_DIM = 128
GDN_CONV = 4
GDN_CHUNK = 64

N_EXPERTS = 32
N_GROUPS = 4
EXPERTS_PER_GROUP = N_EXPERTS // N_GROUPS
TOP_K = 2
MOE_BLOCK = 256

LN_EPS = 1e-5
NORM_EPS = 1e-6
NEG = -1e30

LANES = 128
ROW_TILE = 512
VMEM_LIMIT = 56 * 1024 * 1024


def _params(semantics, **kw):
    return pltpu.CompilerParams(dimension_semantics=semantics, vmem_limit_bytes=VMEM_LIMIT, **kw)


def _dot(a, b):
    return jnp.dot(a, b, preferred_element_type=f32)


def _dot_nt(a, b):
    return lax.dot_general(a, b, (((1,), (1,)), ((), ())), preferred_element_type=f32)


def _dot_tn(a, b):
    return lax.dot_general(a, b, (((0,), (0,)), ((), ())), preferred_element_type=f32)


def _dot_hi(a, b):
    return jnp.dot(a, b, preferred_element_type=f32, precision=lax.Precision.HIGHEST)


def _layer_norm(y, g, b):
    mu = jnp.mean(y, axis=-1, keepdims=True)
    d = y - mu
    var = jnp.mean(d * d, axis=-1, keepdims=True)
    return d * lax.rsqrt(var + LN_EPS) * g + b


def _sigmoid(x):
    return 1.0 / (1.0 + jnp.exp(-x))


def _ab_col_perm():
    perm = np.zeros((3, AB_HEADS * HEAD_DIM), np.int32)
    half = HEAD_DIM // 2
    for t in range(3):
        for col in range(AB_HEADS * HEAD_DIM):
            if t == 2:
                perm[t, col] = t * AB_HEADS * HEAD_DIM + col
                continue
            slab, lane = divmod(col, LANES)
            part, within = divmod(lane, half)
            head = 2 * slab + (part % 2)
            dim = within + half * (part // 2)
            perm[t, col] = t * AB_HEADS * HEAD_DIM + head * HEAD_DIM + dim
    return perm.reshape(-1)


def _ab_proj_kernel(x_ref, w_ref, cos_ref, sin_ref, qa_ref, ka_ref, va_ref, qb_ref, kb_ref, vb_ref, km_ref):
    tm = x_ref.shape[0]
    width = AB_HEADS * HEAD_DIM
    na = N_HEADS_A * HEAD_DIM
    xb = x_ref[...].astype(bf16)
    c = cos_ref[...]
    s = sin_ref[...]
    scale = HEAD_DIM ** -0.5

    def rope(slab):
        return slab * c + pltpu.roll(slab, LANES // 2, 1) * s

    acc = _dot(xb, w_ref[:, 0:width])
    for j in range(width // LANES):
        r = rope(acc[:, j * LANES:(j + 1) * LANES]) * scale
        if j * LANES < na:
            qa_ref[:, j * LANES:(j + 1) * LANES] = r.astype(bf16)
        else:
            qb_ref[:, j * LANES - na:(j + 1) * LANES - na] = r
    acc = _dot(xb, w_ref[:, width:2 * width])
    for j in range(width // LANES):
        r = rope(acc[:, j * LANES:(j + 1) * LANES])
        if j * LANES < na:
            ka_ref[:, j * LANES:(j + 1) * LANES] = r.astype(bf16)
            for blk in range(tm // MOBA_BLOCK):
                km_ref[blk, :, j * LANES:(j + 1) * LANES] = jnp.mean(
                    r[blk * MOBA_BLOCK:(blk + 1) * MOBA_BLOCK], axis=0, keepdims=True)
        else:
            kb_ref[:, j * LANES - na:(j + 1) * LANES - na] = r
    acc = _dot(xb, w_ref[:, 2 * width:3 * width])
    va_ref[...] = acc[:, :na].astype(bf16)
    vb_ref[...] = acc[:, na:]


def _ab_proj(x2d, w_perm, cos_t, sin_t, seq):
    n, d = x2d.shape
    tm = ROW_TILE
    na = N_HEADS_A * HEAD_DIM
    nb = N_HEADS_B * HEAD_DIM
    tiles_per_seq = seq // tm
    row = lambda i: (i, 0)
    out_shape = (
        jax.ShapeDtypeStruct((n, na), bf16), jax.ShapeDtypeStruct((n, na), bf16), jax.ShapeDtypeStruct((n, na), bf16),
        jax.ShapeDtypeStruct((n, nb), f32), jax.ShapeDtypeStruct((n, nb), f32), jax.ShapeDtypeStruct((n, nb), f32),
        jax.ShapeDtypeStruct((n // MOBA_BLOCK, 1, na), f32),
    )
    return pl.pallas_call(
        _ab_proj_kernel,
        grid=(n // tm,),
        in_specs=[
            pl.BlockSpec((tm, d), row),
            pl.BlockSpec(w_perm.shape, lambda i: (0, 0)),
            pl.BlockSpec((tm, LANES), lambda i: (i % tiles_per_seq, 0)),
            pl.BlockSpec((tm, LANES), lambda i: (i % tiles_per_seq, 0)),
        ],
        out_specs=(
            pl.BlockSpec((tm, na), row), pl.BlockSpec((tm, na), row), pl.BlockSpec((tm, na), row),
            pl.BlockSpec((tm, nb), row), pl.BlockSpec((tm, nb), row), pl.BlockSpec((tm, nb), row),
            pl.BlockSpec((tm // MOBA_BLOCK, 1, na), lambda i: (i, 0, 0)),
        ),
        out_shape=out_shape,
        compiler_params=_params(("parallel",)),
        name="ab_proj",
    )(x2d, w_perm, cos_t, sin_t)


def _head_masks():
    lane = lax.broadcasted_iota(i32, (1, LANES), 1)
    qk_h0 = (lane // (HEAD_DIM // 2)) % 2 == 0
    v_h0 = lane < HEAD_DIM
    return qk_h0, v_h0


def _moba_kernel(q_ref, k_ref, v_ref, km_ref, o_ref, thr_ref, m_ref, l_ref, acc_ref):
    i = pl.program_id(2)
    j = pl.program_id(3)
    tq = q_ref.shape[0]
    nblk = km_ref.shape[0]
    qk_h0, v_h0 = _head_masks()

    def q_head(h):
        q = q_ref[...]
        return jnp.where(qk_h0 if h == 0 else jnp.logical_not(qk_h0), q, jnp.zeros_like(q))

    @pl.when(j == 0)
    def _():
        m_ref[...] = jnp.full(m_ref.shape, -jnp.inf, f32)
        l_ref[...] = jnp.zeros(l_ref.shape, f32)
        acc_ref[...] = jnp.zeros(acc_ref.shape, f32)
        km = km_ref[...]
        km_hi = km.astype(bf16)
        km_lo = (km - km_hi.astype(f32)).astype(bf16)
        blk = lax.broadcasted_iota(i32, (tq, nblk), 1)
        for h in range(2):
            qh = q_head(h)
            gate = _dot_nt(qh, km_hi) + _dot_nt(qh, km_lo)
            gate = jnp.where(blk < i, gate, NEG)
            bits = jnp.zeros((tq, 1), i32)
            for _ in range(MOBA_TOPK):
                mx = jnp.max(gate, axis=-1, keepdims=True)
                idx = jnp.min(jnp.where(gate == mx, blk, nblk), axis=-1, keepdims=True)
                bits = bits | jnp.where(idx < i, jnp.left_shift(jnp.ones_like(idx), idx), 0)
                gate = jnp.where(blk == idx, -jnp.inf, gate)
            thr_ref[h] = bits

    @pl.when(j <= i)
    def _():
        own = j == i
        row = lax.broadcasted_iota(i32, (tq, 1), 0)
        col = lax.broadcasted_iota(i32, (tq, k_ref.shape[0]), 1)
        k = k_ref[...]
        v = v_ref[...]
        for h in range(2):
            s = _dot_nt(q_head(h), k)
            sel = (lax.shift_right_logical(thr_ref[h], j) & 1) == 1
            thr = jnp.where(own, row, jnp.where(sel, k_ref.shape[0], -1))
            s = jnp.where(col <= thr, s, NEG)
            m_old = m_ref[h]
            m_new = jnp.maximum(m_old, jnp.max(s, axis=-1, keepdims=True))
            alpha = jnp.exp(m_old - m_new)
            p = jnp.exp(s - m_new)
            l_ref[h] = alpha * l_ref[h] + jnp.sum(p, axis=-1, keepdims=True)
            acc_ref[h] = alpha * acc_ref[h] + _dot(p.astype(bf16), v)
            m_ref[h] = m_new

    @pl.when(j == i)
    def _():
        o0 = acc_ref[0] / l_ref[0]
        o1 = acc_ref[1] / l_ref[1]
        o_ref[...] = jnp.where(v_h0, o0, o1).astype(o_ref.dtype)


def _moba(qa, ka, va, kmean):
    bsz, seq, width = qa.shape
    nblk = seq // MOBA_BLOCK
    assert nblk <= 32, "selection bitmask holds at most 32 key blocks"
    pairs = width // LANES
    t = MOBA_BLOCK
    kv_map = lambda b, p, i, j: (b, jnp.minimum(j, i), p)
    return pl.pallas_call(
        _moba_kernel,
        grid=(bsz, pairs, nblk, nblk),
        in_specs=[
            pl.BlockSpec((None, t, LANES), lambda b, p, i, j: (b, i, p)),
            pl.BlockSpec((None, t, LANES), kv_map),
            pl.BlockSpec((None, t, LANES), kv_map),
            pl.BlockSpec((None, nblk, LANES), lambda b, p, i, j: (b, 0, p)),
        ],
        out_specs=pl.BlockSpec((None, t, LANES), lambda b, p, i, j: (b, i, p)),
        out_shape=jax.ShapeDtypeStruct((bsz, seq, width), bf16),
        scratch_shapes=[
            pltpu.VMEM((2, t, 1), i32),
            pltpu.VMEM((2, t, 1), f32),
            pltpu.VMEM((2, t, 1), f32),
            pltpu.VMEM((2, t, LANES), f32),
        ],
        compiler_params=_params(("parallel", "parallel", "parallel", "arbitrary")),
        name="moba_attention",
    )(qa, ka, va, kmean)


DIL_TILE = 2048


def _dilated_kernel(q_ref, kp_ref, ko_ref, vp_ref, vo_ref, o_ref, kbuf, vbuf, og_ref, lg_ref):
    t = pl.program_id(1)
    tile = q_ref.shape[0]
    w = BAND_BLOCK
    kbuf[0:tile] = kp_ref[...]
    kbuf[tile:2 * tile] = ko_ref[...]
    vbuf[0:tile] = vp_ref[...]
    vbuf[tile:2 * tile] = vo_ref[...]
    qk_h0, v_h0 = _head_masks()
    qi = lax.broadcasted_iota(i32, (w, 2 * w), 0)
    kc = lax.broadcasted_iota(i32, (w, 2 * w), 1)

    def unit(g, qs, ks, dil, first):
        if dil == 1:
            rows_q = pl.ds(qs, w)
            rows_k = pl.ds(ks, 2 * w)
        else:
            rows_q = pl.ds(qs, w, stride=dil)
            rows_k = pl.ds(ks, 2 * w, stride=dil)
        q = q_ref[rows_q, :]
        kk = kbuf[rows_k, :].astype(bf16)
        vv = vbuf[rows_k, :].astype(bf16)
        allowed = (kc >= jnp.maximum(qi, jnp.where(first, w, 0))) & (kc <= qi + w)
        outs, lses = [], []
        for h in range(2):
            qh = jnp.where(qk_h0 if h == 0 else jnp.logical_not(qk_h0), q, 0.0).astype(bf16)
            s = jnp.where(allowed, _dot_nt(qh, kk), NEG)
            m = jnp.max(s, axis=-1, keepdims=True)
            e = jnp.exp(s - m)
            l = jnp.sum(e, axis=-1, keepdims=True)
            outs.append(_dot(e.astype(bf16), vv) / l)
            lses.append(m + jnp.log(l))
        og_ref[g, rows_q, :] = jnp.where(v_h0, outs[0], outs[1])
        lg_ref[g, rows_q, :] = jnp.where(v_h0, lses[0], lses[1])

    first_tile = t == 0
    for g, (window, dil) in enumerate(DILATED_PAIRS):
        assert window // dil == w and tile % (w * dil) == 0
        span = w * dil
        nspan = tile // span

        def span_body(n, carry, g=g, dil=dil, span=span):
            first = first_tile & (n == 0)
            base = pl.multiple_of(n * span, span)
            for r in range(dil):
                unit(g, base + r, tile - span + base + r, dil, first)
            return carry

        if dil <= 4:
            lax.fori_loop(0, nspan, span_body, 0)
        else:
            assert nspan == 1

            def res_body(r, carry, g=g, dil=dil, span=span):
                unit(g, r, tile - span + r, dil, first_tile)
                return carry

            lax.fori_loop(0, dil, res_body, 0)

    l0, l1, l2 = lg_ref[0], lg_ref[1], lg_ref[2]
    mx = jnp.maximum(jnp.maximum(l0, l1), l2)
    e0, e1, e2 = jnp.exp(l0 - mx), jnp.exp(l1 - mx), jnp.exp(l2 - mx)
    den = e0 + e1 + e2
    out = (e0 / den) * og_ref[0] + (e1 / den) * og_ref[1] + (e2 / den) * og_ref[2]
    o_ref[...] = out.astype(o_ref.dtype)


def _dilated(qb, kb, vb):
    bsz, seq, width = qb.shape
    tile = DIL_TILE
    assert seq % tile == 0 and tile == BAND_BLOCK * max(d for _, d in DILATED_PAIRS)
    pairs = width // LANES
    own = lambda b, t, p: (b, t, p)
    prev = lambda b, t, p: (b, jnp.maximum(t - 1, 0), p)
    spec = lambda m: pl.BlockSpec((None, tile, LANES), m)
    return pl.pallas_call(
        _dilated_kernel,
        grid=(bsz, seq // tile, pairs),
        in_specs=[spec(own), spec(prev), spec(own), spec(prev), spec(own)],
        out_specs=spec(own),
        out_shape=jax.ShapeDtypeStruct((bsz, seq, width), bf16),
        scratch_shapes=[
            pltpu.VMEM((2 * tile, LANES), f32),
            pltpu.VMEM((2 * tile, LANES), f32),
            pltpu.VMEM((len(DILATED_PAIRS), tile, LANES), f32),
            pltpu.VMEM((len(DILATED_PAIRS), tile, LANES), f32),
        ],
        compiler_params=_params(("parallel", "parallel", "parallel")),
        name="dilated_attention",
    )(qb, kb, kb, vb, vb)


def _out_proj_ln_kernel(*refs, n_parts, alpha):
    o_refs = refs[:n_parts]
    w_refs = refs[n_parts:2 * n_parts]
    x_ref, g_ref, b_ref, y_ref = refs[2 * n_parts:]
    mix = _dot(o_refs[0][...], w_refs[0][...])
    for o_r, w_r in zip(o_refs[1:], w_refs[1:]):
        mix = mix + _dot(o_r[...], w_r[...])
    y_ref[...] = _layer_norm(alpha * x_ref[...] + mix, g_ref[...], b_ref[...])


def _out_proj_ln(parts, weights, x2d, g, b, alpha):
    n, d = x2d.shape
    tm = ROW_TILE
    row = lambda i: (i, 0)
    const = lambda i: (0, 0)
    in_specs = ([pl.BlockSpec((tm, p.shape[1]), row) for p in parts]
                + [pl.BlockSpec(w.shape, const) for w in weights]
                + [pl.BlockSpec((tm, d), row), pl.BlockSpec((1, d), const), pl.BlockSpec((1, d), const)])
    return pl.pallas_call(
        functools.partial(_out_proj_ln_kernel, n_parts=len(parts), alpha=alpha),
        grid=(n // tm,),
        in_specs=in_specs,
        out_specs=pl.BlockSpec((tm, d), row),
        out_shape=jax.ShapeDtypeStruct((n, d), f32),
        compiler_params=_params(("parallel",)),
        name="out_proj_ln",
    )(*parts, *weights, x2d, g.reshape(1, d), b.reshape(1, d))


def _router_kernel(x_ref, w_ref, bias_ref, idx_ref, gate_ref, cnt_ref, run_ref):
    tm = x_ref.shape[0]
    ne = w_ref.shape[1]

    @pl.when(pl.program_id(0) == 0)
    def _():
        run_ref[...] = jnp.zeros(run_ref.shape, f32)

    scores = _sigmoid(_dot_hi(x_ref[...], w_ref[...]))
    sel = scores + bias_ref[...]
    lane = lax.broadcasted_iota(i32, (tm, ne), 1)
    best = None
    for grp in range(N_GROUPS):
        in_g = (lane >= grp * EXPERTS_PER_GROUP) & (lane < (grp + 1) * EXPERTS_PER_GROUP)
        sg = jnp.where(in_g, sel, -jnp.inf)
        m1 = jnp.max(sg, axis=-1, keepdims=True)
        i1 = jnp.min(jnp.where(sg == m1, lane, ne), axis=-1, keepdims=True)
        sg = jnp.where(lane == i1, -jnp.inf, sg)
        m2 = jnp.max(sg, axis=-1, keepdims=True)
        i2 = jnp.min(jnp.where(sg == m2, lane, ne), axis=-1, keepdims=True)
        gs = m1 + m2
        if best is None:
            best = (gs, i1, i2)
        else:
            better = gs > best[0]
            best = tuple(jnp.where(better, new, old) for new, old in zip((gs, i1, i2), best))
    _, e1, e2 = best
    oh1 = lane == e1
    oh2 = lane == e2
    g1 = jnp.sum(jnp.where(oh1, scores, 0.0), axis=-1, keepdims=True)
    g2 = jnp.sum(jnp.where(oh2, scores, 0.0), axis=-1, keepdims=True)
    gsum = g1 + g2
    both = jnp.where(oh1 | oh2, 1.0, 0.0)
    r_i = lax.broadcasted_iota(i32, (tm, tm), 0)
    c_i = lax.broadcasted_iota(i32, (tm, tm), 1)
    tri = jnp.where(c_i < r_i, 1.0, 0.0).astype(bf16)
    before = _dot(tri, both.astype(bf16)) + run_ref[...]
    r1 = jnp.sum(jnp.where(oh1, before, 0.0), axis=-1, keepdims=True)
    r2 = jnp.sum(jnp.where(oh2, before, 0.0), axis=-1, keepdims=True)
    run_ref[...] = run_ref[...] + jnp.sum(both, axis=0, keepdims=True)
    col = lax.broadcasted_iota(i32, (tm, 4), 1)
    idx_ref[...] = jnp.where(col == 0, e1, jnp.where(col == 1, e2, jnp.where(col == 2, r1.astype(i32), r2.astype(i32))))
    gcol = lax.broadcasted_iota(i32, (tm, 2), 1)
    gate_ref[...] = jnp.where(gcol == 0, g1 / gsum, g2 / gsum)
    cnt_ref[...] = run_ref[...].astype(i32)


def _router(x2d, router_w, router_bias):
    n, d = x2d.shape
    tm = ROW_TILE
    ne = router_w.shape[1]
    return pl.pallas_call(
        _router_kernel,
        grid=(n // tm,),
        in_specs=[
            pl.BlockSpec((tm, d), lambda i: (i, 0)),
            pl.BlockSpec((d, ne), lambda i: (0, 0)),
            pl.BlockSpec((1, ne), lambda i: (0, 0)),
        ],
        out_specs=(
            pl.BlockSpec((tm, 4), lambda i: (i, 0)),
            pl.BlockSpec((tm, 2), lambda i: (i, 0)),
            pl.BlockSpec((1, ne), lambda i: (0, 0)),
        ),
        out_shape=(
            jax.ShapeDtypeStruct((n, 4), i32),
            jax.ShapeDtypeStruct((n, 2), f32),
            jax.ShapeDtypeStruct((1, ne), i32),
        ),
        scratch_shapes=[pltpu.VMEM((1, ne), f32)],
        compiler_params=_params(("arbitrary",)),
        name="moe_router",
    )(x2d, router_w, router_bias.reshape(1, ne))


def _dispatch_kernel(dest_ref, x_hbm, xs_in, xs_hbm, sem):
    del xs_in
    i = pl.program_id(0)
    per_step = dest_ref.shape[0]
    tok0 = i * (per_step // TOP_K)

    def copy(a):
        tok = tok0 + a // TOP_K
        return pltpu.make_async_copy(x_hbm.at[pl.ds(tok, 1)], xs_hbm.at[pl.ds(dest_ref[a], 1)], sem)

    def start(a, c):
        copy(a).start()
        return c

    def wait(a, c):
        copy(a).wait()
        return c

    lax.fori_loop(0, per_step, start, 0)
    lax.fori_loop(0, per_step, wait, 0)


def _dispatch(x2d, dest_flat, cap):
    n, d = x2d.shape
    per_step = 2 * ROW_TILE
    xs0 = jnp.zeros((cap, d), x2d.dtype)
    return pl.pallas_call(
        _dispatch_kernel,
        grid=(n * TOP_K // per_step,),
        in_specs=[
            pl.BlockSpec((per_step,), lambda i: (i,), memory_space=pltpu.SMEM),
            pl.BlockSpec(memory_space=pl.ANY),
            pl.BlockSpec(memory_space=pl.ANY),
        ],
        out_specs=pl.BlockSpec(memory_space=pl.ANY),
        out_shape=jax.ShapeDtypeStruct((cap, d), x2d.dtype),
        scratch_shapes=[pltpu.SemaphoreType.DMA(())],
        input_output_aliases={2: 0},
        compiler_params=_params(("arbitrary",), has_side_effects=True),
        name="moe_dispatch",
    )(dest_flat, x2d, xs0)


def _expert_kernel(be_ref, nused_ref, xs_ref, w1_ref, w3_ref, w2_ref, ys_ref, w1b, w3b, w2b):
    b = pl.program_id(0)
    changed = jnp.logical_or(b == 0, be_ref[b] != be_ref[jnp.maximum(b - 1, 0)])

    @pl.when(changed)
    def _():
        w1b[...] = w1_ref[...].astype(bf16)
        w3b[...] = w3_ref[...].astype(bf16)
        w2b[...] = w2_ref[...].astype(bf16)

    @pl.when(b < nused_ref[0])
    def _():
        xb = xs_ref[...].astype(bf16)
        h1 = _dot(xb, w1b[...])
        h3 = _dot(xb, w3b[...])
        hb = (h1 * _sigmoid(h1)) * h3
        ys_ref[...] = _dot(hb.astype(bf16), w2b[...])

    @pl.when(b >= nused_ref[0])
    def _():
        ys_ref[...] = jnp.zeros(ys_ref.shape, f32)


def _expert_ffn(xs, block_expert, n_used, w1, w3, w2):
    cap, d = xs.shape
    ne, _, dff = w1.shape
    nblocks = cap // MOE_BLOCK
    grid_spec = pltpu.PrefetchScalarGridSpec(
        num_scalar_prefetch=2,
        grid=(nblocks,),
        in_specs=[
            pl.BlockSpec((MOE_BLOCK, d), lambda b, be, nu: (b, 0)),
            pl.BlockSpec((None, d, dff), lambda b, be, nu: (be[b], 0, 0)),
            pl.BlockSpec((None, d, dff), lambda b, be, nu: (be[b], 0, 0)),
            pl.BlockSpec((None, dff, d), lambda b, be, nu: (be[b], 0, 0)),
        ],
        out_specs=pl.BlockSpec((MOE_BLOCK, d), lambda b, be, nu: (b, 0)),
        scratch_shapes=[pltpu.VMEM((d, dff), bf16), pltpu.VMEM((d, dff), bf16), pltpu.VMEM((dff, d), bf16)],
    )
    return pl.pallas_call(
        _expert_kernel,
        grid_spec=grid_spec,
        out_shape=jax.ShapeDtypeStruct((cap, d), f32),
        compiler_params=_params(("arbitrary",)),
        name="moe_expert_ffn",
    )(block_expert, n_used, xs, w1, w3, w2)


def _combine_kernel(dest_ref, x_ref, gate_ref, g_ref, b_ref, ys_hbm, y_ref, buf, sem, *, alpha):
    tm = x_ref.shape[0]

    def copy(a):
        return pltpu.make_async_copy(ys_hbm.at[pl.ds(dest_ref[a], 1)], buf.at[a % TOP_K, pl.ds(a // TOP_K, 1)], sem)

    def start(a, c):
        copy(a).start()
        return c

    def wait(a, c):
        copy(a).wait()
        return c

    lax.fori_loop(0, tm * TOP_K, start, 0)
    lax.fori_loop(0, tm * TOP_K, wait, 0)
    gates = gate_ref[...]
    ffn = buf[0] * gates[:, 0:1] + buf[1] * gates[:, 1:2]
    y_ref[...] = _layer_norm(alpha * x_ref[...] + ffn, g_ref[...], b_ref[...])


def _combine_ln(x2d, ys, dest_flat, gates, g, b, alpha):
    n, d = x2d.shape
    tm = ROW_TILE
    return pl.pallas_call(
        functools.partial(_combine_kernel, alpha=alpha),
        grid=(n // tm,),
        in_specs=[
            pl.BlockSpec((tm * TOP_K,), lambda i: (i,), memory_space=pltpu.SMEM),
            pl.BlockSpec((tm, d), lambda i: (i, 0)),
            pl.BlockSpec((tm, TOP_K), lambda i: (i, 0)),
            pl.BlockSpec((1, d), lambda i: (0, 0)),
            pl.BlockSpec((1, d), lambda i: (0, 0)),
            pl.BlockSpec(memory_space=pl.ANY),
        ],
        out_specs=pl.BlockSpec((tm, d), lambda i: (i, 0)),
        out_shape=jax.ShapeDtypeStruct((n, d), f32),
        scratch_shapes=[pltpu.VMEM((TOP_K, tm, d), f32), pltpu.SemaphoreType.DMA(())],
        compiler_params=_params(("arbitrary",)),
        name="moe_combine_ln",
    )(dest_flat, x2d, gates, g.reshape(1, d), b.reshape(1, d), ys)


def _moe_ln(x2d, router_w, router_bias, w1, w3, w2, g, b, alpha):
    n, d = x2d.shape
    idx, gates, counts = _router(x2d, router_w, router_bias)
    counts = counts.reshape(-1)
    padded = (counts + MOE_BLOCK - 1) // MOE_BLOCK * MOE_BLOCK
    pad_end = jnp.cumsum(padded)
    pad_start = pad_end - padded
    nblocks = n * TOP_K // MOE_BLOCK + N_EXPERTS
    cap = nblocks * MOE_BLOCK
    block_expert = jnp.minimum(
        jnp.searchsorted(pad_end, jnp.arange(nblocks, dtype=i32) * MOE_BLOCK, side='right'), N_EXPERTS - 1).astype(i32)
    n_used = (pad_end[-1:] // MOE_BLOCK).astype(i32)
    dest_flat = (pad_start[idx[:, :TOP_K]] + idx[:, TOP_K:]).reshape(-1).astype(i32)
    xs = _dispatch(x2d, dest_flat, cap)
    ys = _expert_ffn(xs, block_expert, n_used, w1, w3, w2)
    return _combine_ln(x2d, ys, dest_flat, gates, g, b, alpha)


def _gdn_proj_kernel(x_ref, wc_ref, wz_ref, wba_ref, cw_ref, alog_ref, dt_ref,
                     qkv_ref, z_ref, beta_ref, gc_ref, buf, *, tiles_per_seq):
    i = pl.program_id(0)
    tm = x_ref.shape[0]
    halo = 8
    xb = x_ref[...].astype(bf16)
    qk_cols = 2 * GDN_QK_HEADS * GDN_HEAD_DIM
    conv_ch = wc_ref.shape[1]

    @pl.when(i % tiles_per_seq == 0)
    def _():
        buf[0:halo, :] = jnp.zeros((halo, conv_ch), f32)

    buf[halo:halo + tm, :] = _dot(xb, wc_ref[...])
    q_scale = GDN_HEAD_DIM ** -0.5
    for c in range(conv_ch // LANES):
        cols = slice(c * LANES, (c + 1) * LANES)
        y = jnp.zeros((tm, LANES), f32)
        for j in range(GDN_CONV):
            shift = GDN_CONV - 1 - j
            y = y + buf[halo - shift:halo - shift + tm, cols] * cw_ref[j:j + 1, cols]
        y = y * _sigmoid(y)
        if c * LANES < qk_cols:
            y = y * lax.rsqrt(jnp.sum(y * y, axis=-1, keepdims=True) + NORM_EPS)
            if c * LANES < qk_cols // 2:
                y = y * q_scale
        qkv_ref[:, cols] = y
    buf[0:halo, :] = buf[tm:tm + halo, :]

    z_ref[...] = _dot(xb, wz_ref[...])
    ba = _dot(xb, wba_ref[...])
    hv = GDN_V_HEADS
    beta_ref[...] = _sigmoid(ba[:, :hv])
    a = ba[:, hv:] + dt_ref[...]
    softplus = jnp.maximum(a, 0.0) + jnp.log(1.0 + jnp.exp(-jnp.abs(a)))
    g = -jnp.exp(alog_ref[...]) * softplus
    r_i = lax.broadcasted_iota(i32, (tm, tm), 0)
    c_i = lax.broadcasted_iota(i32, (tm, tm), 1)
    same_chunk = (r_i // GDN_CHUNK) == (c_i // GDN_CHUNK)
    cum = jnp.where(same_chunk & (c_i <= r_i), 1.0, 0.0)
    gc_ref[...] = _dot_hi(cum, g)


def _gdn_proj(x2d, w_conv, w_z, w_ba, conv_w, a_log, dt_bias, seq):
    n, d = x2d.shape
    tm = ROW_TILE // 2
    conv_ch = w_conv.shape[1]
    vdim = w_z.shape[1]
    hv = GDN_V_HEADS
    row = lambda i: (i, 0)
    const = lambda i: (0, 0)
    return pl.pallas_call(
        functools.partial(_gdn_proj_kernel, tiles_per_seq=seq // tm),
        grid=(n // tm,),
        in_specs=[
            pl.BlockSpec((tm, d), row),
            pl.BlockSpec(w_conv.shape, const, pipeline_mode=pl.Buffered(1)),
            pl.BlockSpec(w_z.shape, const, pipeline_mode=pl.Buffered(1)),
            pl.BlockSpec(w_ba.shape, const),
            pl.BlockSpec(conv_w.shape, const),
            pl.BlockSpec((1, hv), const),
            pl.BlockSpec((1, hv), const),
        ],
        out_specs=(
            pl.BlockSpec((tm, conv_ch), row),
            pl.BlockSpec((tm, vdim), row),
            pl.BlockSpec((tm, hv), row),
            pl.BlockSpec((tm, hv), row),
        ),
        out_shape=(
            jax.ShapeDtypeStruct((n, conv_ch), f32),
            jax.ShapeDtypeStruct((n, vdim), f32),
            jax.ShapeDtypeStruct((n, hv), f32),
            jax.ShapeDtypeStruct((n, hv), f32),
        ),
        scratch_shapes=[pltpu.VMEM((tm + 16, conv_ch), f32)],
        compiler_params=_params(("arbitrary",)),
        name="gdn_proj",
    )(x2d, w_conv, w_z, w_ba, conv_w, a_log.reshape(1, hv), dt_bias.reshape(1, hv))


def _gdn_delta_kernel(q_ref, k_ref, v_ref, z_ref, beta_ref, gcc_ref, gcr_ref, ng_ref, o_ref, state):
    ts = q_ref.shape[0]
    c = GDN_CHUNK

    @pl.when(pl.program_id(2) == 0)
    def _():
        state[...] = jnp.zeros(state.shape, f32)

    r_i = lax.broadcasted_iota(i32, (c, c), 0)
    c_i = lax.broadcasted_iota(i32, (c, c), 1)
    lower = r_i >= c_i
    strict = r_i > c_i
    for n in range(ts // c):
        rows = slice(n * c, (n + 1) * c)
        q = q_ref[rows, :]
        k = k_ref[rows, :]
        v = v_ref[rows, :]
        beta = beta_ref[rows, :]
        gcc = gcc_ref[rows, :]
        gcr = gcr_ref[n]
        g_last = gcr[:, c - 1:c]
        decay = jnp.where(lower, jnp.exp(jnp.where(lower, gcc - gcr, 0.0)), 0.0)
        egc = jnp.exp(gcc)
        kb = k * beta
        k16 = k.astype(bf16)
        l_mat = jnp.where(strict, _dot_nt(kb.astype(bf16), k16) * decay, 0.0)
        rhs = jnp.concatenate([v * beta, kb * egc], axis=-1)
        sol = rhs - _dot_hi(l_mat, rhs)
        pw = _dot_hi(l_mat, l_mat)
        span = 2
        while span < c:
            sol = sol + _dot_hi(pw, sol)
            span *= 2
            if span < c:
                pw = _dot_hi(pw, pw)
        dv = v.shape[1]
        u = sol[:, :dv]
        w = sol[:, dv:]
        attn = _dot_nt(q.astype(bf16), k16) * decay
        q_dec = q * egc
        k_dec = k * jnp.exp(g_last - gcc)
        s_old = state[...]
        s16 = s_old.astype(bf16)
        v_new = u - _dot(w.astype(bf16), s16)
        v16 = v_new.astype(bf16)
        o = _dot(q_dec.astype(bf16), s16) + _dot(attn.astype(bf16), v16)
        state[...] = s_old * jnp.exp(g_last) + _dot_tn(k_dec.astype(bf16), v16)
        z = z_ref[rows, :]
        o = o * lax.rsqrt(jnp.mean(o * o, axis=-1, keepdims=True) + NORM_EPS) * ng_ref[...] * (z * _sigmoid(z))
        o_ref[rows, :] = o.astype(o_ref.dtype)


def _gdn_delta(qkv, z, beta_col, gc_col, gc_row, norm_g, ts=256):
    bsz, seq, _ = qkv.shape
    hd = GDN_HEAD_DIM
    rep = GDN_V_HEADS // GDN_QK_HEADS
    c = GDN_CHUNK
    col_spec = pl.BlockSpec((None, None, ts, 1), lambda b, h, s: (b, h, s, 0))
    return pl.pallas_call(
        _gdn_delta_kernel,
        grid=(bsz, GDN_V_HEADS, seq // ts),
        in_specs=[
            pl.BlockSpec((None, ts, hd), lambda b, h, s: (b, s, h // rep)),
            pl.BlockSpec((None, ts, hd), lambda b, h, s: (b, s, GDN_QK_HEADS + h // rep)),
            pl.BlockSpec((None, ts, hd), lambda b, h, s: (b, s, 2 * GDN_QK_HEADS + h)),
            pl.BlockSpec((None, ts, hd), lambda b, h, s: (b, s, h)),
            col_spec,
            col_spec,
            pl.BlockSpec((None, None, ts // c, 1, c), lambda b, h, s: (b, h, s, 0, 0)),
            pl.BlockSpec((1, hd), lambda b, h, s: (0, 0)),
        ],
        out_specs=pl.BlockSpec((None, ts, hd), lambda b, h, s: (b, s, h)),
        out_shape=jax.ShapeDtypeStruct((bsz, seq, GDN_V_HEADS * hd), bf16),
        scratch_shapes=[pltpu.VMEM((hd, hd), f32)],
        compiler_params=_params(("parallel", "parallel", "arbitrary")),
        name="gdn_delta",
    )(qkv, qkv, qkv, z, beta_col, gc_col, gc_row, norm_g.reshape(1, hd))


def _rope_slab_tables(seq):
    half = HEAD_DIM // 2
    inv = ROPE_THETA ** (-jnp.arange(0, HEAD_DIM, 2, dtype=f32) / HEAD_DIM)
    ang = jnp.arange(seq, dtype=f32)[:, None] * inv[None, :]
    cos, sin = jnp.cos(ang), jnp.sin(ang)
    cos_t = jnp.concatenate([cos] * (LANES // half), axis=1)
    sin_t = jnp.concatenate([-sin, -sin, sin, sin], axis=1)
    return cos_t, sin_t


def _attention_layer(x2d, bsz, seq, w_in, w_out, cos_t, sin_t, g, b, alpha):
    n, d = x2d.shape
    w_perm = w_in[:, _ab_col_perm()].astype(bf16)
    qa, ka, va, qb, kb, vb, kmean = _ab_proj(x2d, w_perm, cos_t, sin_t, seq)
    sh = lambda t: t.reshape(bsz, seq, t.shape[-1])
    o_a = _moba(sh(qa), sh(ka), sh(va), kmean.reshape(bsz, seq // MOBA_BLOCK, -1))
    o_b = _dilated(sh(qb), sh(kb), sh(vb))
    na = N_HEADS_A * HEAD_DIM
    w_o = w_out.astype(bf16)
    return _out_proj_ln([o_a.reshape(n, -1), o_b.reshape(n, -1)], [w_o[:na], w_o[na:]], x2d, g, b, alpha)


def _gdn_layer(x2d, bsz, seq, w_in, conv_w, a_log, dt_bias, norm_g, w_out, g, b, alpha):
    n, d = x2d.shape
    conv_ch = conv_w.shape[1]
    vdim = GDN_V_HEADS * GDN_HEAD_DIM
    w16 = w_in.astype(bf16)
    qkv, z, beta, gc = _gdn_proj(x2d, w16[:, :conv_ch], w16[:, conv_ch:conv_ch + vdim], w16[:, conv_ch + vdim:],
                                 conv_w, a_log, dt_bias, seq)
    heads_major = lambda t: jnp.swapaxes(t.reshape(bsz, seq, GDN_V_HEADS), 1, 2)
    beta_col = heads_major(beta)[..., None]
    gc_hm = heads_major(gc)
    gc_col = gc_hm[..., None]
    gc_row = gc_hm.reshape(bsz, GDN_V_HEADS, seq // GDN_CHUNK, 1, GDN_CHUNK)
    o = _gdn_delta(qkv.reshape(bsz, seq, -1), z.reshape(bsz, seq, -1), beta_col, gc_col, gc_row, norm_g)
    return _out_proj_ln([o.reshape(n, -1)], [w_out.astype(bf16)], x2d, g, b, alpha)


def kernel(x, ab_w_in, ab_w_out, gdn_w_in, gdn_conv_w, gdn_a_log, gdn_dt_bias, gdn_norm_g, gdn_w_out, mix_ln_g, mix_ln_b, router_w, router_bias, moe_w1, moe_w3, moe_w2, ffn_ln_g, ffn_ln_b):
    bsz, seq, d = x.shape
    depth = mix_ln_g.shape[0]
    alpha = (2.0 * depth) ** 0.25
    cos_t, sin_t = _rope_slab_tables(seq)
    h = x.reshape(bsz * seq, d)
    for layer in range(depth):
        j = layer // 2
        if layer % 2 == 0:
            h = _attention_layer(h, bsz, seq, ab_w_in[j], ab_w_out[j], cos_t, sin_t,
                                 mix_ln_g[layer], mix_ln_b[layer], alpha)
        else:
            h = _gdn_layer(h, bsz, seq, gdn_w_in[j], gdn_conv_w[j], gdn_a_log[j], gdn_dt_bias[j], gdn_norm_g[j],
                           gdn_w_out[j], mix_ln_g[layer], mix_ln_b[layer], alpha)
        h = _moe_ln(h, router_w, router_bias, moe_w1[layer], moe_w3[layer], moe_w2[layer],
                    ffn_ln_g[layer], ffn_ln_b[layer], alpha)
    return h.reshape(bsz, seq, d)
```

```python
import functools

import numpy as np
import jax
import jax.numpy as jnp
from jax import lax
from jax.experimental import pallas as pl
from jax.experimental.pallas import tpu as pltpu

f32 = jnp.float32
bf16 = jnp.bfloat16
i32 = jnp.int32

HEAD_DIM = 64
N_HEADS_A = 8
N_HEADS_B = 8
AB_HEADS = N_HEADS_A + N_HEADS_B
MOBA_BLOCK = 256
MOBA_TOPK = 3
DILATED_PAIRS = ((128, 1), (512, 4), (2048, 16))
BAND_BLOCK = 128
ROPE_THETA = 10000.0

GDN_QK_HEADS = 8
GDN_V_HEADS = 16
GDN_HEAD_DIM = 128
GDN_CONV = 4
GDN_CHUNK = 64

N_EXPERTS = 32
N_GROUPS = 4
EXPERTS_PER_GROUP = N_EXPERTS // N_GROUPS
TOP_K = 2
MOE_BLOCK = 256

LN_EPS = 1e-5
NORM_EPS = 1e-6
NEG = -1e30

LANES = 128
ROW_TILE = 512
GDN_TILE = 256
VMEM_LIMIT = 56 * 1024 * 1024


def _params(semantics, **kw):
    return pltpu.CompilerParams(dimension_semantics=semantics, vmem_limit_bytes=VMEM_LIMIT, **kw)


def _dot(a, b):
    return jnp.dot(a, b, preferred_element_type=f32)


def _dot_nt(a, b):
    return lax.dot_general(a, b, (((1,), (1,)), ((), ())), preferred_element_type=f32)


def _dot_tn(a, b):
    return lax.dot_general(a, b, (((0,), (0,)), ((), ())), preferred_element_type=f32)


def _dot_hi(a, b):
    return jnp.dot(a, b, preferred_element_type=f32, precision=lax.Precision.HIGHEST)


def _split(a):
    hi = a.astype(bf16)
    return hi, (a - hi.astype(f32)).astype(bf16)


def _dot3(a, b):
    return _dot(a[0], b[0]) + (_dot(a[0], b[1]) + _dot(a[1], b[0]))


def _layer_norm(y, g, b):
    mu = jnp.mean(y, axis=-1, keepdims=True)
    d = y - mu
    var = jnp.mean(d * d, axis=-1, keepdims=True)
    return d * lax.rsqrt(var + LN_EPS) * g + b


def _sigmoid(x):
    return 1.0 / (1.0 + jnp.exp(-x))


def _ab_col_perm():
    perm = np.zeros((3, AB_HEADS * HEAD_DIM), np.int32)
    half = HEAD_DIM // 2
    for t in range(3):
        for col in range(AB_HEADS * HEAD_DIM):
            if t == 2:
                perm[t, col] = t * AB_HEADS * HEAD_DIM + col
                continue
            slab, lane = divmod(col, LANES)
            part, within = divmod(lane, half)
            head = 2 * slab + (part % 2)
            dim = within + half * (part // 2)
            perm[t, col] = t * AB_HEADS * HEAD_DIM + head * HEAD_DIM + dim
    return perm.reshape(-1)


def _ab_proj_kernel(x_ref, w_ref, cos_ref, sin_ref, qa_ref, ka_ref, vat_ref, qb_ref, kb_ref, vb_ref, km_ref):
    tm = x_ref.shape[0]
    width = AB_HEADS * HEAD_DIM
    na = N_HEADS_A * HEAD_DIM
    xb = x_ref[...].astype(bf16)
    c = cos_ref[...]
    s = sin_ref[...]
    scale = HEAD_DIM ** -0.5

    def rope(slab):
        return slab * c + pltpu.roll(slab, LANES // 2, 1) * s

    acc = _dot(xb, w_ref[:, 0:width])
    for j in range(width // LANES):
        r = rope(acc[:, j * LANES:(j + 1) * LANES]) * scale
        if j * LANES < na:
            qa_ref[:, j * LANES:(j + 1) * LANES] = r.astype(bf16)
        else:
            qb_ref[:, j * LANES - na:(j + 1) * LANES - na] = r
    acc = _dot(xb, w_ref[:, width:2 * width])
    for j in range(width // LANES):
        r = rope(acc[:, j * LANES:(j + 1) * LANES])
        if j * LANES < na:
            ka_ref[:, j * LANES:(j + 1) * LANES] = r.astype(bf16)
            for blk in range(tm // MOBA_BLOCK):
                km_ref[blk, :, j * LANES:(j + 1) * LANES] = jnp.mean(
                    r[blk * MOBA_BLOCK:(blk + 1) * MOBA_BLOCK], axis=0, keepdims=True)
        else:
            kb_ref[:, j * LANES - na:(j + 1) * LANES - na] = r
    acc = _dot(xb, w_ref[:, 2 * width:3 * width])
    for blk in range(tm // MOBA_BLOCK):
        vat_ref[blk] = acc[blk * MOBA_BLOCK:(blk + 1) * MOBA_BLOCK, :na].T.astype(bf16)
    vb_ref[...] = acc[:, na:]


def _ab_proj(x2d, w_perm, cos_t, sin_t, seq):
    n, d = x2d.shape
    tm = ROW_TILE
    na = N_HEADS_A * HEAD_DIM
    nb = N_HEADS_B * HEAD_DIM
    tiles_per_seq = seq // tm
    row = lambda i: (i, 0)
    out_shape = (
        jax.ShapeDtypeStruct((n, na), bf16), jax.ShapeDtypeStruct((n, na), bf16),
        jax.ShapeDtypeStruct((n // MOBA_BLOCK, na, MOBA_BLOCK), bf16),
        jax.ShapeDtypeStruct((n, nb), f32), jax.ShapeDtypeStruct((n, nb), f32), jax.ShapeDtypeStruct((n, nb), f32),
        jax.ShapeDtypeStruct((n // MOBA_BLOCK, 1, na), f32),
    )
    return pl.pallas_call(
        _ab_proj_kernel,
        grid=(n // tm,),
        in_specs=[
            pl.BlockSpec((tm, d), row),
            pl.BlockSpec(w_perm.shape, lambda i: (0, 0)),
            pl.BlockSpec((tm, LANES), lambda i: (i % tiles_per_seq, 0)),
            pl.BlockSpec((tm, LANES), lambda i: (i % tiles_per_seq, 0)),
        ],
        out_specs=(
            pl.BlockSpec((tm, na), row), pl.BlockSpec((tm, na), row),
            pl.BlockSpec((tm // MOBA_BLOCK, na, MOBA_BLOCK), lambda i: (i, 0, 0)),
            pl.BlockSpec((tm, nb), row), pl.BlockSpec((tm, nb), row), pl.BlockSpec((tm, nb), row),
            pl.BlockSpec((tm // MOBA_BLOCK, 1, na), lambda i: (i, 0, 0)),
        ),
        out_shape=out_shape,
        compiler_params=_params(("parallel",)),
        name="ab_proj",
    )(x2d, w_perm, cos_t, sin_t)


def _head_masks():
    lane = lax.broadcasted_iota(i32, (1, LANES), 1)
    qk_h0 = (lane // (HEAD_DIM // 2)) % 2 == 0
    v_h0 = lane < HEAD_DIM
    return qk_h0, v_h0


def _moba_kernel(q_ref, k_ref, vt_ref, km_ref, o_ref, sel_ref, qh_ref, sa_ref, sb_ref, *state):
    i = pl.program_id(2)
    tq = q_ref.shape[0]
    tk = MOBA_BLOCK
    nblk = km_ref.shape[0]
    hd = HEAD_DIM
    qk_h0, _ = _head_masks()
    q = q_ref[...]
    qh = (jnp.where(qk_h0, q, jnp.zeros_like(q)), jnp.where(qk_h0, jnp.zeros_like(q), q))

    km = km_ref[...]
    km_hi = km.astype(bf16)
    km_lo = (km - km_hi.astype(f32)).astype(bf16)
    blk = lax.broadcasted_iota(i32, (nblk, tq), 0)
    for h in range(2):
        gate = _dot_nt(km_hi, qh[h]) + _dot_nt(km_lo, qh[h])
        gate = jnp.where(blk < i, gate, NEG)
        sel = jnp.zeros((nblk, tq), f32)
        for _ in range(MOBA_TOPK):
            mx = jnp.max(gate, axis=0, keepdims=True)
            idx = jnp.min(jnp.where(gate == mx, blk, nblk), axis=0, keepdims=True)
            sel = jnp.where(blk == jnp.where(idx < i, idx, -1), 1.0, sel)
            gate = jnp.where(blk == idx, -jnp.inf, gate)
        sel_ref[h] = sel

    halves = tq // LANES
    subs = [(h, c, state[3 * (h * halves + c):3 * (h * halves + c) + 3]) for h in range(2) for c in range(halves)]
    for _, _, (m_ref, l_ref, acc_ref) in subs:
        m_ref[...] = jnp.full(m_ref.shape, -jnp.inf, f32)
        l_ref[...] = jnp.zeros(l_ref.shape, f32)
        acc_ref[...] = jnp.zeros(acc_ref.shape, f32)
    for h in range(2):
        qh_ref[h] = qh[h]
    k_io = lax.broadcasted_iota(i32, (tk, LANES), 0)
    q_io = lax.broadcasted_iota(i32, (tk, LANES), 1)

    def scores(j, s_ref):
        kj = k_ref[pl.ds(pl.multiple_of(j * tk, tk), tk), :]
        for h in range(2):
            s_ref[h] = _dot_nt(kj, qh_ref[h])

    def softmax_pv(j, s_ref, own):
        vtj = vt_ref[j]
        sel_rows = None if own else [sel_ref[h, pl.ds(j, 1), :] for h in range(2)]
        probs = []
        for h, c, (m_ref, l_ref, _) in subs:
            cols = slice(c * LANES, (c + 1) * LANES)
            s = s_ref[h, :, cols]
            allowed = (k_io <= q_io + c * LANES) if own else (sel_rows[h][:, cols] > 0.5)
            s = jnp.where(allowed, s, NEG)
            m = m_ref[...]
            m_new = jnp.maximum(m, jnp.max(s, axis=0, keepdims=True))
            alpha = jnp.exp(m - m_new)
            p = jnp.exp(s - m_new)
            l_ref[...] = alpha * l_ref[...] + jnp.sum(p, axis=0, keepdims=True)
            m_ref[...] = m_new
            probs.append((alpha, p.astype(bf16)))
        for (h, c, (_, _, acc_ref)), (alpha, p16) in zip(subs, probs):
            acc_ref[...] = alpha * acc_ref[...] + _dot(vtj[h * hd:(h + 1) * hd, :], p16)

    scores(0, sa_ref)

    def body(p, carry):
        j = 2 * p
        scores(j + 1, sb_ref)
        softmax_pv(j, sa_ref, False)
        scores(j + 2, sa_ref)
        softmax_pv(j + 1, sb_ref, False)
        return carry

    lax.fori_loop(0, i // 2, body, 0)

    @pl.when(i % 2 == 0)
    def _():
        softmax_pv(i, sa_ref, True)

    @pl.when(i % 2 == 1)
    def _():
        scores(i, sb_ref)
        softmax_pv(i - 1, sa_ref, False)
        softmax_pv(i, sb_ref, True)

    o_t = jnp.concatenate(
        [jnp.concatenate([subs[h * halves + c][2][2][...] / subs[h * halves + c][2][1][...] for c in range(halves)], axis=1)
         for h in range(2)], axis=0)
    o_ref[...] = o_t.T.astype(o_ref.dtype)


def _moba(qa, ka, vat, kmean):
    bsz, seq, width = qa.shape
    nblk = seq // MOBA_BLOCK
    pairs = width // LANES
    t = MOBA_BLOCK
    return pl.pallas_call(
        _moba_kernel,
        grid=(bsz, pairs, nblk),
        in_specs=[
            pl.BlockSpec((None, t, LANES), lambda b, p, i: (b, i, p)),
            pl.BlockSpec((None, seq, LANES), lambda b, p, i: (b, 0, p)),
            pl.BlockSpec((None, nblk, LANES, t), lambda b, p, i: (b, 0, p, 0)),
            pl.BlockSpec((None, nblk, LANES), lambda b, p, i: (b, 0, p)),
        ],
        out_specs=pl.BlockSpec((None, t, LANES), lambda b, p, i: (b, i, p)),
        out_shape=jax.ShapeDtypeStruct((bsz, seq, width), bf16),
        scratch_shapes=[pltpu.VMEM((2, nblk, t), f32), pltpu.VMEM((2, t, LANES), bf16),
                        pltpu.VMEM((2, t, t), f32), pltpu.VMEM((2, t, t), f32)] + 2 * (t // LANES) * [
            pltpu.VMEM((1, LANES), f32), pltpu.VMEM((1, LANES), f32), pltpu.VMEM((HEAD_DIM, LANES), f32)],
        compiler_params=_params(("parallel", "parallel", "arbitrary")),
        name="moba_attention",
    )(qa, ka, vat, kmean)


DIL_TILE = 2048


def _dilated_kernel(q_ref, kp_ref, ko_ref, vp_ref, vo_ref, o_ref, kbuf, vbuf, og_ref, lg_ref):
    t = pl.program_id(1)
    tile = q_ref.shape[0]
    w = BAND_BLOCK
    kbuf[0:tile] = kp_ref[...]
    kbuf[tile:2 * tile] = ko_ref[...]
    vbuf[0:tile] = vp_ref[...]
    vbuf[tile:2 * tile] = vo_ref[...]
    qk_h0, v_h0 = _head_masks()
    qi = lax.broadcasted_iota(i32, (w, 2 * w), 0)
    kc = lax.broadcasted_iota(i32, (w, 2 * w), 1)

    def unit(g, qs, ks, dil, first):
        if dil == 1:
            rows_q = pl.ds(qs, w)
            rows_k = pl.ds(ks, 2 * w)
        else:
            rows_q = pl.ds(qs, w, stride=dil)
            rows_k = pl.ds(ks, 2 * w, stride=dil)
        q = q_ref[rows_q, :]
        kk = kbuf[rows_k, :].astype(bf16)
        vv = vbuf[rows_k, :].astype(bf16)
        allowed = (kc >= jnp.maximum(qi, jnp.where(first, w, 0))) & (kc <= qi + w)
        outs, lses = [], []
        for h in range(2):
            qh = jnp.where(qk_h0 if h == 0 else jnp.logical_not(qk_h0), q, 0.0).astype(bf16)
            s = jnp.where(allowed, _dot_nt(qh, kk), NEG)
            m = jnp.max(s, axis=-1, keepdims=True)
            e = jnp.exp(s - m)
            l = jnp.sum(e, axis=-1, keepdims=True)
            outs.append(_dot(e.astype(bf16), vv) / l)
            lses.append(m + jnp.log(l))
        og_ref[g, rows_q, :] = jnp.where(v_h0, outs[0], outs[1])
        lg_ref[g, rows_q, :] = jnp.where(v_h0, lses[0], lses[1])

    first_tile = t == 0
    for g, (window, dil) in enumerate(DILATED_PAIRS):
        assert window // dil == w and tile % (w * dil) == 0
        span = w * dil
        nspan = tile // span

        def span_body(n, carry, g=g, dil=dil, span=span):
            first = first_tile & (n == 0)
            base = pl.multiple_of(n * span, span)
            for r in range(dil):
                unit(g, base + r, tile - span + base + r, dil, first)
            return carry

        if dil <= 4:
            lax.fori_loop(0, nspan, span_body, 0)
        else:
            assert nspan == 1

            def res_body(r, carry, g=g, dil=dil, span=span):
                unit(g, r, tile - span + r, dil, first_tile)
                return carry

            lax.fori_loop(0, dil, res_body, 0)

    l0, l1, l2 = lg_ref[0], lg_ref[1], lg_ref[2]
    mx = jnp.maximum(jnp.maximum(l0, l1), l2)
    e0, e1, e2 = jnp.exp(l0 - mx), jnp.exp(l1 - mx), jnp.exp(l2 - mx)
    den = e0 + e1 + e2
    out = (e0 / den) * og_ref[0] + (e1 / den) * og_ref[1] + (e2 / den) * og_ref[2]
    o_ref[...] = out.astype(o_ref.dtype)


def _dilated(qb, kb, vb):
    bsz, seq, width = qb.shape
    tile = DIL_TILE
    assert seq % tile == 0 and tile == BAND_BLOCK * max(d for _, d in DILATED_PAIRS)
    pairs = width // LANES
    own = lambda b, t, p: (b, t, p)
    prev = lambda b, t, p: (b, jnp.maximum(t - 1, 0), p)
    spec = lambda m: pl.BlockSpec((None, tile, LANES), m)
    return pl.pallas_call(
        _dilated_kernel,
        grid=(bsz, seq // tile, pairs),
        in_specs=[spec(own), spec(prev), spec(own), spec(prev), spec(own)],
        out_specs=spec(own),
        out_shape=jax.ShapeDtypeStruct((bsz, seq, width), bf16),
        scratch_shapes=[
            pltpu.VMEM((2 * tile, LANES), f32),
            pltpu.VMEM((2 * tile, LANES), f32),
            pltpu.VMEM((len(DILATED_PAIRS), tile, LANES), f32),
            pltpu.VMEM((len(DILATED_PAIRS), tile, LANES), f32),
        ],
        compiler_params=_params(("parallel", "parallel", "parallel")),
        name="dilated_attention",
    )(qb, kb, kb, vb, vb)


def _out_proj_ln_kernel(*refs, n_parts, alpha):
    o_refs = refs[:n_parts]
    w_refs = refs[n_parts:2 * n_parts]
    x_ref, g_ref, b_ref, y_ref = refs[2 * n_parts:]
    mix = _dot(o_refs[0][...], w_refs[0][...])
    for o_r, w_r in zip(o_refs[1:], w_refs[1:]):
        mix = mix + _dot(o_r[...], w_r[...])
    y_ref[...] = _layer_norm(alpha * x_ref[...] + mix, g_ref[...], b_ref[...])


def _out_proj_ln(parts, weights, x2d, g, b, alpha):
    n, d = x2d.shape
    tm = ROW_TILE
    row = lambda i: (i, 0)
    const = lambda i: (0, 0)
    in_specs = ([pl.BlockSpec((tm, p.shape[1]), row) for p in parts]
                + [pl.BlockSpec(w.shape, const) for w in weights]
                + [pl.BlockSpec((tm, d), row), pl.BlockSpec((1, d), const), pl.BlockSpec((1, d), const)])
    return pl.pallas_call(
        functools.partial(_out_proj_ln_kernel, n_parts=len(parts), alpha=alpha),
        grid=(n // tm,),
        in_specs=in_specs,
        out_specs=pl.BlockSpec((tm, d), row),
        out_shape=jax.ShapeDtypeStruct((n, d), f32),
        compiler_params=_params(("parallel",)),
        name="out_proj_ln",
    )(*parts, *weights, x2d, g.reshape(1, d), b.reshape(1, d))


def _router_kernel(x_ref, w_ref, bias_ref, idx_ref, gate_ref, cnt_ref, run_ref):
    tm = x_ref.shape[0]
    ne = w_ref.shape[1]

    @pl.when(pl.program_id(0) == 0)
    def _():
        run_ref[...] = jnp.zeros(run_ref.shape, f32)

    scores = _sigmoid(_dot_hi(x_ref[...], w_ref[...]))
    sel = scores + bias_ref[...]
    lane = lax.broadcasted_iota(i32, (tm, ne), 1)
    best = None
    for grp in range(N_GROUPS):
        in_g = (lane >= grp * EXPERTS_PER_GROUP) & (lane < (grp + 1) * EXPERTS_PER_GROUP)
        sg = jnp.where(in_g, sel, -jnp.inf)
        m1 = jnp.max(sg, axis=-1, keepdims=True)
        i1 = jnp.min(jnp.where(sg == m1, lane, ne), axis=-1, keepdims=True)
        sg = jnp.where(lane == i1, -jnp.inf, sg)
        m2 = jnp.max(sg, axis=-1, keepdims=True)
        i2 = jnp.min(jnp.where(sg == m2, lane, ne), axis=-1, keepdims=True)
        gs = m1 + m2
        if best is None:
            best = (gs, i1, i2)
        else:
            better = gs > best[0]
            best = tuple(jnp.where(better, new, old) for new, old in zip((gs, i1, i2), best))
    _, e1, e2 = best
    oh1 = lane == e1
    oh2 = lane == e2
    g1 = jnp.sum(jnp.where(oh1, scores, 0.0), axis=-1, keepdims=True)
    g2 = jnp.sum(jnp.where(oh2, scores, 0.0), axis=-1, keepdims=True)
    gsum = g1 + g2
    both = jnp.where(oh1 | oh2, 1.0, 0.0)
    r_i = lax.broadcasted_iota(i32, (tm, tm), 0)
    c_i = lax.broadcasted_iota(i32, (tm, tm), 1)
    tri = jnp.where(c_i < r_i, 1.0, 0.0).astype(bf16)
    before = _dot(tri, both.astype(bf16)) + run_ref[...]
    r1 = jnp.sum(jnp.where(oh1, before, 0.0), axis=-1, keepdims=True)
    r2 = jnp.sum(jnp.where(oh2, before, 0.0), axis=-1, keepdims=True)
    run_ref[...] = run_ref[...] + jnp.sum(both, axis=0, keepdims=True)
    col = lax.broadcasted_iota(i32, (tm, 4), 1)
    idx_ref[...] = jnp.where(col == 0, e1, jnp.where(col == 1, e2, jnp.where(col == 2, r1.astype(i32), r2.astype(i32))))
    gcol = lax.broadcasted_iota(i32, (tm, 2), 1)
    gate_ref[...] = jnp.where(gcol == 0, g1 / gsum, g2 / gsum)
    cnt_ref[...] = run_ref[...].astype(i32)


def _router(x2d, router_w, router_bias):
    n, d = x2d.shape
    tm = ROW_TILE
    ne = router_w.shape[1]
    return pl.pallas_call(
        _router_kernel,
        grid=(n // tm,),
        in_specs=[
            pl.BlockSpec((tm, d), lambda i: (i, 0)),
            pl.BlockSpec((d, ne), lambda i: (0, 0)),
            pl.BlockSpec((1, ne), lambda i: (0, 0)),
        ],
        out_specs=(
            pl.BlockSpec((tm, 4), lambda i: (i, 0)),
            pl.BlockSpec((tm, 2), lambda i: (i, 0)),
            pl.BlockSpec((1, ne), lambda i: (0, 0)),
        ),
        out_shape=(
            jax.ShapeDtypeStruct((n, 4), i32),
            jax.ShapeDtypeStruct((n, 2), f32),
            jax.ShapeDtypeStruct((1, ne), i32),
        ),
        scratch_shapes=[pltpu.VMEM((1, ne), f32)],
        compiler_params=_params(("arbitrary",)),
        name="moe_router",
    )(x2d, router_w, router_bias.reshape(1, ne))


def _dispatch_kernel(dest_ref, x_ref, xs_in, xs_hbm, sem):
    del xs_in
    per_step = dest_ref.shape[0]

    def copy(t, k):
        return pltpu.make_async_copy(x_ref.at[pl.ds(t, 1)], xs_hbm.at[pl.ds(dest_ref[TOP_K * t + k], 1)], sem)

    def start(t, c):
        for k in range(TOP_K):
            copy(t, k).start()
        return c

    def wait(t, c):
        for k in range(TOP_K):
            copy(t, k).wait()
        return c

    lax.fori_loop(0, per_step // TOP_K, start, 0)
    lax.fori_loop(0, per_step // TOP_K, wait, 0)


def _dispatch(x2d, dest_flat, cap):
    n, d = x2d.shape
    per_step = 2 * ROW_TILE
    xs0 = jnp.zeros((cap, d), x2d.dtype)
    return pl.pallas_call(
        _dispatch_kernel,
        grid=(n * TOP_K // per_step,),
        in_specs=[
            pl.BlockSpec((per_step,), lambda i: (i,), memory_space=pltpu.SMEM),
            pl.BlockSpec((per_step // TOP_K, d), lambda i: (i, 0)),
            pl.BlockSpec(memory_space=pl.ANY),
        ],
        out_specs=pl.BlockSpec(memory_space=pl.ANY),
        out_shape=jax.ShapeDtypeStruct((cap, d), x2d.dtype),
        scratch_shapes=[pltpu.SemaphoreType.DMA(())],
        input_output_aliases={2: 0},
        compiler_params=_params(("arbitrary",), has_side_effects=True, disable_bounds_checks=True),
        name="moe_dispatch",
    )(dest_flat, x2d, xs0)


def _expert_kernel(be_ref, nused_ref, xs_ref, w1_ref, w3_ref, w2_ref, ys_ref, w1b, w3b, w2b):
    b = pl.program_id(0)
    changed = jnp.logical_or(b == 0, be_ref[b] != be_ref[jnp.maximum(b - 1, 0)])

    @pl.when(changed)
    def _():
        w1b[...] = w1_ref[...].astype(bf16)
        w3b[...] = w3_ref[...].astype(bf16)
        w2b[...] = w2_ref[...].astype(bf16)

    @pl.when(b < nused_ref[0])
    def _():
        xb = xs_ref[...].astype(bf16)
        h1 = _dot(xb, w1b[...])
        h3 = _dot(xb, w3b[...])
        hb = (h1 * _sigmoid(h1)) * h3
        ys_ref[...] = _dot(hb.astype(bf16), w2b[...])

    @pl.when(b >= nused_ref[0])
    def _():
        ys_ref[...] = jnp.zeros(ys_ref.shape, f32)


def _expert_ffn(xs, block_expert, n_used, w1, w3, w2):
    cap, d = xs.shape
    ne, _, dff = w1.shape
    nblocks = cap // MOE_BLOCK
    grid_spec = pltpu.PrefetchScalarGridSpec(
        num_scalar_prefetch=2,
        grid=(nblocks,),
        in_specs=[
            pl.BlockSpec((MOE_BLOCK, d), lambda b, be, nu: (b, 0)),
            pl.BlockSpec((None, d, dff), lambda b, be, nu: (be[b], 0, 0)),
            pl.BlockSpec((None, d, dff), lambda b, be, nu: (be[b], 0, 0)),
            pl.BlockSpec((None, dff, d), lambda b, be, nu: (be[b], 0, 0)),
        ],
        out_specs=pl.BlockSpec((MOE_BLOCK, d), lambda b, be, nu: (b, 0)),
        scratch_shapes=[pltpu.VMEM((d, dff), bf16), pltpu.VMEM((d, dff), bf16), pltpu.VMEM((dff, d), bf16)],
    )
    return pl.pallas_call(
        _expert_kernel,
        grid_spec=grid_spec,
        out_shape=jax.ShapeDtypeStruct((cap, d), f32),
        compiler_params=_params(("arbitrary",)),
        name="moe_expert_ffn",
    )(block_expert, n_used, xs, w1, w3, w2)


def _combine_kernel(dest_ref, x_ref, gate_ref, g_ref, b_ref, ys_hbm, y_ref, buf, sem, *, alpha):
    tm = x_ref.shape[0]

    def copy(t, k):
        return pltpu.make_async_copy(ys_hbm.at[pl.ds(dest_ref[TOP_K * t + k], 1)], buf.at[k, pl.ds(t, 1)], sem)

    def start(t, c):
        for k in range(TOP_K):
            copy(t, k).start()
        return c

    def wait(t, c):
        for k in range(TOP_K):
            copy(t, k).wait()
        return c

    lax.fori_loop(0, tm, start, 0)
    lax.fori_loop(0, tm, wait, 0)
    gates = gate_ref[...]
    ffn = buf[0] * gates[:, 0:1] + buf[1] * gates[:, 1:2]
    y_ref[...] = _layer_norm(alpha * x_ref[...] + ffn, g_ref[...], b_ref[...])


def _combine_ln(x2d, ys, dest_flat, gates, g, b, alpha):
    n, d = x2d.shape
    tm = ROW_TILE
    return pl.pallas_call(
        functools.partial(_combine_kernel, alpha=alpha),
        grid=(n // tm,),
        in_specs=[
            pl.BlockSpec((tm * TOP_K,), lambda i: (i,), memory_space=pltpu.SMEM),
            pl.BlockSpec((tm, d), lambda i: (i, 0)),
            pl.BlockSpec((tm, TOP_K), lambda i: (i, 0)),
            pl.BlockSpec((1, d), lambda i: (0, 0)),
            pl.BlockSpec((1, d), lambda i: (0, 0)),
            pl.BlockSpec(memory_space=pl.ANY),
        ],
        out_specs=pl.BlockSpec((tm, d), lambda i: (i, 0)),
        out_shape=jax.ShapeDtypeStruct((n, d), f32),
        scratch_shapes=[pltpu.VMEM((TOP_K, tm, d), f32), pltpu.SemaphoreType.DMA(())],
        compiler_params=_params(("arbitrary",), disable_bounds_checks=True),
        name="moe_combine_ln",
    )(dest_flat, x2d, gates, g.reshape(1, d), b.reshape(1, d), ys)


def _moe_ln(x2d, router_w, router_bias, w1, w3, w2, g, b, alpha):
    n, d = x2d.shape
    idx, gates, counts = _router(x2d, router_w, router_bias)
    counts = counts.reshape(-1)
    padded = (counts + MOE_BLOCK - 1) // MOE_BLOCK * MOE_BLOCK
    pad_end = jnp.cumsum(padded)
    pad_start = pad_end - padded
    nblocks = n * TOP_K // MOE_BLOCK + N_EXPERTS
    cap = nblocks * MOE_BLOCK
    block_expert = jnp.minimum(
        jnp.searchsorted(pad_end, jnp.arange(nblocks, dtype=i32) * MOE_BLOCK, side='right'), N_EXPERTS - 1).astype(i32)
    n_used = (pad_end[-1:] // MOE_BLOCK).astype(i32)
    dest_flat = (pad_start[idx[:, :TOP_K]] + idx[:, TOP_K:]).reshape(-1).astype(i32)
    xs = _dispatch(x2d, dest_flat, cap)
    ys = _expert_ffn(xs, block_expert, n_used, w1, w3, w2)
    return _combine_ln(x2d, ys, dest_flat, gates, g, b, alpha)


def _gdn_proj_kernel(x_ref, wc_ref, wz_ref, wba_ref, cw_ref, alog_ref, dt_ref,
                     qkv_ref, z_ref, beta_ref, gc_ref, buf, *, tiles_per_seq):
    i = pl.program_id(0)
    tm = x_ref.shape[0]
    halo = 8
    xb = x_ref[...].astype(bf16)
    qk_cols = 2 * GDN_QK_HEADS * GDN_HEAD_DIM
    conv_ch = wc_ref.shape[1]

    @pl.when(i % tiles_per_seq == 0)
    def _():
        buf[0:halo, :] = jnp.zeros((halo, conv_ch), f32)

    buf[halo:halo + tm, :] = _dot(xb, wc_ref[...])
    q_scale = GDN_HEAD_DIM ** -0.5
    for c in range(conv_ch // LANES):
        cols = slice(c * LANES, (c + 1) * LANES)
        y = jnp.zeros((tm, LANES), f32)
        for j in range(GDN_CONV):
            shift = GDN_CONV - 1 - j
            y = y + buf[halo - shift:halo - shift + tm, cols] * cw_ref[j:j + 1, cols]
        y = y * _sigmoid(y)
        if c * LANES < qk_cols:
            y = y * lax.rsqrt(jnp.sum(y * y, axis=-1, keepdims=True) + NORM_EPS)
            if c * LANES < qk_cols // 2:
                y = y * q_scale
        qkv_ref[:, cols] = y
    buf[0:halo, :] = buf[tm:tm + halo, :]

    z_ref[...] = _dot(xb, wz_ref[...])
    ba = _dot(xb, wba_ref[...])
    hv = GDN_V_HEADS
    beta_ref[...] = _sigmoid(ba[:, :hv])
    a = ba[:, hv:] + dt_ref[...]
    softplus = jnp.maximum(a, 0.0) + jnp.log(1.0 + jnp.exp(-jnp.abs(a)))
    g = -jnp.exp(alog_ref[...]) * softplus
    r_i = lax.broadcasted_iota(i32, (tm, tm), 0)
    c_i = lax.broadcasted_iota(i32, (tm, tm), 1)
    same_chunk = (r_i // GDN_CHUNK) == (c_i // GDN_CHUNK)
    cum = jnp.where(same_chunk & (c_i <= r_i), 1.0, 0.0)
    gc_ref[...] = _dot_hi(cum, g)


def _gdn_proj(x2d, w_conv, w_z, w_ba, conv_w, a_log, dt_bias, seq):
    n, d = x2d.shape
    tm = ROW_TILE // 2
    conv_ch = w_conv.shape[1]
    vdim = w_z.shape[1]
    hv = GDN_V_HEADS
    row = lambda i: (i, 0)
    const = lambda i: (0, 0)
    return pl.pallas_call(
        functools.partial(_gdn_proj_kernel, tiles_per_seq=seq // tm),
        grid=(n // tm,),
        in_specs=[
            pl.BlockSpec((tm, d), row),
            pl.BlockSpec(w_conv.shape, const, pipeline_mode=pl.Buffered(1)),
            pl.BlockSpec(w_z.shape, const, pipeline_mode=pl.Buffered(1)),
            pl.BlockSpec(w_ba.shape, const),
            pl.BlockSpec(conv_w.shape, const),
            pl.BlockSpec((1, hv), const),
            pl.BlockSpec((1, hv), const),
        ],
        out_specs=(
            pl.BlockSpec((tm, conv_ch), row),
            pl.BlockSpec((tm, vdim), row),
            pl.BlockSpec((tm, hv), row),
            pl.BlockSpec((tm, hv), row),
        ),
        out_shape=(
            jax.ShapeDtypeStruct((n, conv_ch), f32),
            jax.ShapeDtypeStruct((n, vdim), f32),
            jax.ShapeDtypeStruct((n, hv), f32),
            jax.ShapeDtypeStruct((n, hv), f32),
        ),
        scratch_shapes=[pltpu.VMEM((tm + 16, conv_ch), f32)],
        compiler_params=_params(("arbitrary",)),
        name="gdn_proj",
    )(x2d, w_conv, w_z, w_ba, conv_w, a_log.reshape(1, hv), dt_bias.reshape(1, hv))


def _gdn_delta_kernel(q_ref, k_ref, v_ref, z_ref, beta_ref, gcc_ref, gcr_ref, ng_ref, o_ref, state):
    ts = q_ref.shape[0]
    c = GDN_CHUNK

    @pl.when(pl.program_id(2) == 0)
    def _():
        state[...] = jnp.zeros(state.shape, f32)

    hb = state.shape[0]
    hd = GDN_HEAD_DIM
    nch = ts // c
    r_i = lax.broadcasted_iota(i32, (ts, ts), 0)
    c_i = lax.broadcasted_iota(i32, (ts, ts), 1)
    same = (r_i // c) == (c_i // c)
    lower = same & (r_i >= c_i)
    strict = same & (r_i > c_i)
    q = q_ref[...]
    k = k_ref[...]
    k16 = k.astype(bf16)
    qk = _dot_nt(q.astype(bf16), k16)

    hs = range(hb)
    beta = [beta_ref[h] for h in hs]
    gcc = [gcc_ref[h] for h in hs]
    gcr = [gcr_ref[h] for h in hs]
    decay = [jnp.where(lower, jnp.exp(jnp.where(lower, gcc[h] - gcr[h], 0.0)), 0.0) for h in hs]
    egc = [jnp.exp(gcc[h]) for h in hs]
    kb = [k * beta[h] for h in hs]
    kk = [_dot_nt(kb[h].astype(bf16), k16) for h in hs]
    l_s = [_split(jnp.where(strict, kk[h] * decay[h], 0.0)) for h in hs]
    rhs = [jnp.concatenate([v_ref[:, h * hd:(h + 1) * hd] * beta[h], kb[h] * egc[h]], axis=-1) for h in hs]
    r_s = [_split(rhs[h]) for h in hs]
    lr = [_dot3(l_s[h], r_s[h]) for h in hs]
    pw = [_dot3(l_s[h], l_s[h]) for h in hs]
    sol = [rhs[h] - lr[h] for h in hs]
    span = 2
    while span < c:
        p_s = [_split(pw[h]) for h in hs]
        s_s = [_split(sol[h]) for h in hs]
        upd = [_dot3(p_s[h], s_s[h]) for h in hs]
        span *= 2
        if span < c:
            pw = [_dot3(p_s[h], p_s[h]) for h in hs]
        sol = [sol[h] + upd[h] for h in hs]
    sol16 = [sol[h].astype(bf16) for h in hs]
    auw = [_dot((qk * decay[h]).astype(bf16), sol16[h]) for h in hs]
    heads = []
    for h in hs:
        o0 = auw[h][:, :hd]
        qe16 = (q * egc[h] - auw[h][:, hd:]).astype(bf16)
        ab, cd = [], []
        for n in range(nch):
            rows = slice(n * c, (n + 1) * c)
            g_last = gcr[h][:, (n + 1) * c - 1:(n + 1) * c]
            kd16 = (k[rows] * jnp.exp(g_last - gcc[h][rows])).astype(bf16)
            ab.append(_dot_tn(kd16, sol16[h][rows]))
            cd.append(jnp.exp(g_last))
        heads.append((o0, qe16, ab, cd))

    outs = [[] for _ in range(hb)]
    s_cur = [state[h] for h in range(hb)]
    for n in range(nch):
        rows = slice(n * c, (n + 1) * c)
        for h in range(hb):
            o0, qe16, ab, cd = heads[h]
            s16 = s_cur[h].astype(bf16)
            outs[h].append(_dot(qe16[rows], s16) + o0[rows])
            s_cur[h] = s_cur[h] * cd[n] - _dot(ab[n][:, hd:].astype(bf16), s16) + ab[n][:, :hd]
    for h in range(hb):
        state[h] = s_cur[h]
        o = jnp.concatenate(outs[h], axis=0)
        z = z_ref[:, h * hd:(h + 1) * hd]
        o = o * lax.rsqrt(jnp.mean(o * o, axis=-1, keepdims=True) + NORM_EPS) * ng_ref[...] * (z * _sigmoid(z))
        o_ref[:, h * hd:(h + 1) * hd] = o.astype(o_ref.dtype)


def _gdn_delta(qkv, z, beta_col, gc_col, gc_row, norm_g, ts):
    bsz, seq, _ = qkv.shape
    hd = GDN_HEAD_DIM
    rep = GDN_V_HEADS // GDN_QK_HEADS
    col_spec = pl.BlockSpec((None, rep, ts, 1), lambda b, g, s: (b, g, s, 0))
    v_off = 2 * GDN_QK_HEADS // rep
    return pl.pallas_call(
        _gdn_delta_kernel,
        grid=(bsz, GDN_QK_HEADS, seq // ts),
        in_specs=[
            pl.BlockSpec((None, ts, hd), lambda b, g, s: (b, s, g)),
            pl.BlockSpec((None, ts, hd), lambda b, g, s: (b, s, GDN_QK_HEADS + g)),
            pl.BlockSpec((None, ts, rep * hd), lambda b, g, s: (b, s, v_off + g)),
            pl.BlockSpec((None, ts, rep * hd), lambda b, g, s: (b, s, g)),
            col_spec,
            col_spec,
            pl.BlockSpec((None, rep, None, 1, ts), lambda b, g, s: (b, g, s, 0, 0)),
            pl.BlockSpec((1, hd), lambda b, g, s: (0, 0)),
        ],
        out_specs=pl.BlockSpec((None, ts, rep * hd), lambda b, g, s: (b, s, g)),
        out_shape=jax.ShapeDtypeStruct((bsz, seq, GDN_V_HEADS * hd), bf16),
        scratch_shapes=[pltpu.VMEM((rep, hd, hd), f32)],
        compiler_params=_params(("parallel", "parallel", "arbitrary")),
        name="gdn_delta",
    )(qkv, qkv, qkv, z, beta_col, gc_col, gc_row, norm_g.reshape(1, hd))


def _rope_slab_tables(seq):
    half = HEAD_DIM // 2
    inv = ROPE_THETA ** (-jnp.arange(0, HEAD_DIM, 2, dtype=f32) / HEAD_DIM)
    ang = jnp.arange(seq, dtype=f32)[:, None] * inv[None, :]
    cos, sin = jnp.cos(ang), jnp.sin(ang)
    cos_t = jnp.concatenate([cos] * (LANES // half), axis=1)
    sin_t = jnp.concatenate([-sin, -sin, sin, sin], axis=1)
    return cos_t, sin_t


def _attention_layer(x2d, bsz, seq, w_in, w_out, cos_t, sin_t, g, b, alpha):
    n, d = x2d.shape
    w_perm = w_in[:, _ab_col_perm()].astype(bf16)
    qa, ka, vat, qb, kb, vb, kmean = _ab_proj(x2d, w_perm, cos_t, sin_t, seq)
    sh = lambda t: t.reshape(bsz, seq, t.shape[-1])
    nblk = seq // MOBA_BLOCK
    o_a = _moba(sh(qa), sh(ka), vat.reshape(bsz, nblk, -1, MOBA_BLOCK), kmean.reshape(bsz, nblk, -1))
    o_b = _dilated(sh(qb), sh(kb), sh(vb))
    na = N_HEADS_A * HEAD_DIM
    w_o = w_out.astype(bf16)
    return _out_proj_ln([o_a.reshape(n, -1), o_b.reshape(n, -1)], [w_o[:na], w_o[na:]], x2d, g, b, alpha)


def _gdn_layer(x2d, bsz, seq, w_in, conv_w, a_log, dt_bias, norm_g, w_out, g, b, alpha):
    n, d = x2d.shape
    conv_ch = conv_w.shape[1]
    vdim = GDN_V_HEADS * GDN_HEAD_DIM
    w16 = w_in.astype(bf16)
    qkv, z, beta, gc = _gdn_proj(x2d, w16[:, :conv_ch], w16[:, conv_ch:conv_ch + vdim], w16[:, conv_ch + vdim:],
                                 conv_w, a_log, dt_bias, seq)
    heads_major = lambda t: jnp.swapaxes(t.reshape(bsz, seq, GDN_V_HEADS), 1, 2)
    beta_col = heads_major(beta)[..., None]
    gc_hm = heads_major(gc)
    gc_col = gc_hm[..., None]
    gc_row = gc_hm.reshape(bsz, GDN_V_HEADS, seq // GDN_TILE, 1, GDN_TILE)
    o = _gdn_delta(qkv.reshape(bsz, seq, -1), z.reshape(bsz, seq, -1), beta_col, gc_col, gc_row, norm_g, GDN_TILE)
    return _out_proj_ln([o.reshape(n, -1)], [w_out.astype(bf16)], x2d, g, b, alpha)


def kernel(x, ab_w_in, ab_w_out, gdn_w_in, gdn_conv_w, gdn_a_log, gdn_dt_bias, gdn_norm_g, gdn_w_out, mix_ln_g, mix_ln_b, router_w, router_bias, moe_w1, moe_w3, moe_w2, ffn_ln_g, ffn_ln_b):
    bsz, seq, d = x.shape
    depth = mix_ln_g.shape[0]
    alpha = (2.0 * depth) ** 0.25
    cos_t, sin_t = _rope_slab_tables(seq)
    h = x.reshape(bsz * seq, d)
    for layer in range(depth):
        j = layer // 2
        if layer % 2 == 0:
            h = _attention_layer(h, bsz, seq, ab_w_in[j], ab_w_out[j], cos_t, sin_t,
                                 mix_ln_g[layer], mix_ln_b[layer], alpha)
        else:
            h = _gdn_layer(h, bsz, seq, gdn_w_in[j], gdn_conv_w[j], gdn_a_log[j], gdn_dt_bias[j], gdn_norm_g[j],
                           gdn_w_out[j], mix_ln_g[layer], mix_ln_b[layer], alpha)
        h = _moe_ln(h, router_w, router_bias, moe_w1[layer], moe_w3[layer], moe_w2[layer],
                    ffn_ln_g[layer], ffn_ln_b[layer], alpha)
    return h.reshape(bsz, seq, d)
```

```python
import functools

import numpy as np
import jax
import jax.numpy as jnp
from jax import lax
from jax.experimental import pallas as pl
from jax.experimental.pallas import tpu as pltpu

f32 = jnp.float32
bf16 = jnp.bfloat16
i32 = jnp.int32

HEAD_DIM = 64
N_HEADS_A = 8
N_HEADS_B = 8
AB_HEADS = N_HEADS_A + N_HEADS_B
MOBA_BLOCK = 256
MOBA_TOPK = 3
DILATED_PAIRS = ((128, 1), (512, 4), (2048, 16))
BAND_BLOCK = 128
ROPE_THETA = 10000.0

GDN_QK_HEADS = 8
GDN_V_HEADS = 16
GDN_HEAD_DIM = 128
GDN_CONV = 4
GDN_CHUNK = 64

N_EXPERTS = 32
N_GROUPS = 4
EXPERTS_PER_GROUP = N_EXPERTS // N_GROUPS
TOP_K = 2
MOE_BLOCK = 256

LN_EPS = 1e-5
NORM_EPS = 1e-6
NEG = -1e30

LANES = 128
SUBLANES = 8
ROW_TILE = 512
GDN_SOLVE = 256
GDN_TILE = 1024
VMEM_LIMIT = 56 * 1024 * 1024


def _params(semantics, **kw):
    return pltpu.CompilerParams(dimension_semantics=semantics, vmem_limit_bytes=VMEM_LIMIT, **kw)


def _dot(a, b):
    return jnp.dot(a, b, preferred_element_type=f32)


def _dot_nt(a, b):
    return lax.dot_general(a, b, (((1,), (1,)), ((), ())), preferred_element_type=f32)


def _dot_tn(a, b):
    return lax.dot_general(a, b, (((0,), (0,)), ((), ())), preferred_element_type=f32)


def _dot_hi(a, b):
    return jnp.dot(a, b, preferred_element_type=f32, precision=lax.Precision.HIGHEST)


def _split(a):
    hi = a.astype(bf16)
    return hi, (a - hi.astype(f32)).astype(bf16)


def _layer_norm(y, g, b):
    mu = jnp.mean(y, axis=-1, keepdims=True)
    d = y - mu
    var = jnp.mean(d * d, axis=-1, keepdims=True)
    return d * lax.rsqrt(var + LN_EPS) * g + b


def _sigmoid(x):
    return 1.0 / (1.0 + jnp.exp(-x))


def _ab_col_perm():
    perm = np.zeros((3, AB_HEADS * HEAD_DIM), np.int32)
    half = HEAD_DIM // 2
    for t in range(3):
        for col in range(AB_HEADS * HEAD_DIM):
            if t == 2:
                perm[t, col] = t * AB_HEADS * HEAD_DIM + col
                continue
            slab, lane = divmod(col, LANES)
            part, within = divmod(lane, half)
            head = 2 * slab + (part % 2)
            dim = within + half * (part // 2)
            perm[t, col] = t * AB_HEADS * HEAD_DIM + head * HEAD_DIM + dim
    return perm.reshape(-1)


def _ab_proj_kernel(x_ref, w_ref, cos_ref, sin_ref, qa_ref, ka_ref, vat_ref, qb_ref, kb_ref, vb_ref, km_ref):
    tm = x_ref.shape[0]
    width = AB_HEADS * HEAD_DIM
    na = N_HEADS_A * HEAD_DIM
    xb = x_ref[...].astype(bf16)
    c = cos_ref[...]
    s = sin_ref[...]
    scale = HEAD_DIM ** -0.5

    def rope(slab):
        return slab * c + pltpu.roll(slab, LANES // 2, 1) * s

    acc = _dot(xb, w_ref[:, 0:width])
    for j in range(width // LANES):
        r = rope(acc[:, j * LANES:(j + 1) * LANES]) * scale
        if j * LANES < na:
            qa_ref[:, j * LANES:(j + 1) * LANES] = r.astype(bf16)
        else:
            qb_ref[:, j * LANES - na:(j + 1) * LANES - na] = r
    acc = _dot(xb, w_ref[:, width:2 * width])
    for j in range(width // LANES):
        r = rope(acc[:, j * LANES:(j + 1) * LANES])
        if j * LANES < na:
            ka_ref[:, j * LANES:(j + 1) * LANES] = r.astype(bf16)
            for blk in range(tm // MOBA_BLOCK):
                km_ref[blk, :, j * LANES:(j + 1) * LANES] = jnp.mean(
                    r[blk * MOBA_BLOCK:(blk + 1) * MOBA_BLOCK], axis=0, keepdims=True)
        else:
            kb_ref[:, j * LANES - na:(j + 1) * LANES - na] = r
    acc = _dot(xb, w_ref[:, 2 * width:3 * width])
    for blk in range(tm // MOBA_BLOCK):
        vat_ref[blk] = acc[blk * MOBA_BLOCK:(blk + 1) * MOBA_BLOCK, :na].T.astype(bf16)
    vb_ref[...] = acc[:, na:]


def _ab_proj(x2d, w_perm, cos_t, sin_t, seq):
    n, d = x2d.shape
    tm = ROW_TILE
    na = N_HEADS_A * HEAD_DIM
    nb = N_HEADS_B * HEAD_DIM
    tiles_per_seq = seq // tm
    row = lambda i: (i, 0)
    out_shape = (
        jax.ShapeDtypeStruct((n, na), bf16), jax.ShapeDtypeStruct((n, na), bf16),
        jax.ShapeDtypeStruct((n // MOBA_BLOCK, na, MOBA_BLOCK), bf16),
        jax.ShapeDtypeStruct((n, nb), f32), jax.ShapeDtypeStruct((n, nb), f32), jax.ShapeDtypeStruct((n, nb), f32),
        jax.ShapeDtypeStruct((n // MOBA_BLOCK, 1, na), f32),
    )
    return pl.pallas_call(
        _ab_proj_kernel,
        grid=(n // tm,),
        in_specs=[
            pl.BlockSpec((tm, d), row),
            pl.BlockSpec(w_perm.shape, lambda i: (0, 0)),
            pl.BlockSpec((tm, LANES), lambda i: (i % tiles_per_seq, 0)),
            pl.BlockSpec((tm, LANES), lambda i: (i % tiles_per_seq, 0)),
        ],
        out_specs=(
            pl.BlockSpec((tm, na), row), pl.BlockSpec((tm, na), row),
            pl.BlockSpec((tm // MOBA_BLOCK, na, MOBA_BLOCK), lambda i: (i, 0, 0)),
            pl.BlockSpec((tm, nb), row), pl.BlockSpec((tm, nb), row), pl.BlockSpec((tm, nb), row),
            pl.BlockSpec((tm // MOBA_BLOCK, 1, na), lambda i: (i, 0, 0)),
        ),
        out_shape=out_shape,
        compiler_params=_params(("parallel",)),
        name="ab_proj",
    )(x2d, w_perm, cos_t, sin_t)


def _head_masks():
    lane = lax.broadcasted_iota(i32, (1, LANES), 1)
    qk_h0 = (lane // (HEAD_DIM // 2)) % 2 == 0
    v_h0 = lane < HEAD_DIM
    return qk_h0, v_h0


def _moba_kernel(q_ref, k_ref, vt_ref, km_ref, o_ref, sel_ref, qh_ref, sa_ref, sb_ref, *state):
    i = pl.program_id(2)
    tq = q_ref.shape[0]
    tk = MOBA_BLOCK
    nblk = km_ref.shape[0]
    hd = HEAD_DIM
    qk_h0, _ = _head_masks()
    q = q_ref[...]
    qh = (jnp.where(qk_h0, q, jnp.zeros_like(q)), jnp.where(qk_h0, jnp.zeros_like(q), q))

    km = km_ref[...]
    km_hi = km.astype(bf16)
    km_lo = (km - km_hi.astype(f32)).astype(bf16)
    blk = lax.broadcasted_iota(i32, (nblk, tq), 0)
    for h in range(2):
        gate = _dot_nt(km_hi, qh[h]) + _dot_nt(km_lo, qh[h])
        gate = jnp.where(blk < i, gate, NEG)
        sel = jnp.zeros((nblk, tq), f32)
        for _ in range(MOBA_TOPK):
            mx = jnp.max(gate, axis=0, keepdims=True)
            idx = jnp.min(jnp.where(gate == mx, blk, nblk), axis=0, keepdims=True)
            sel = jnp.where(blk == jnp.where(idx < i, idx, -1), 1.0, sel)
            gate = jnp.where(blk == idx, -jnp.inf, gate)
        sel_ref[h] = sel

    halves = tq // LANES
    subs = [(h, c, state[3 * (h * halves + c):3 * (h * halves + c) + 3]) for h in range(2) for c in range(halves)]
    for _, _, (m_ref, l_ref, acc_ref) in subs:
        m_ref[...] = jnp.full(m_ref.shape, -jnp.inf, f32)
        l_ref[...] = jnp.zeros(l_ref.shape, f32)
        acc_ref[...] = jnp.zeros(acc_ref.shape, f32)
    for h in range(2):
        qh_ref[h] = qh[h]
    k_io = lax.broadcasted_iota(i32, (tk, LANES), 0)
    q_io = lax.broadcasted_iota(i32, (tk, LANES), 1)

    def scores(j, s_ref):
        kj = k_ref[pl.ds(pl.multiple_of(j * tk, tk), tk), :]
        for h in range(2):
            s_ref[h] = _dot_nt(kj, qh_ref[h])

    def softmax_pv(j, s_ref, own):
        vtj = vt_ref[j]
        sel_rows = None if own else [sel_ref[h, pl.ds(j, 1), :] for h in range(2)]
        probs = []
        for h, c, (m_ref, l_ref, _) in subs:
            cols = slice(c * LANES, (c + 1) * LANES)
            s = s_ref[h, :, cols]
            allowed = (k_io <= q_io + c * LANES) if own else (sel_rows[h][:, cols] > 0.5)
            s = jnp.where(allowed, s, NEG)
            m = m_ref[...]
            m_new = jnp.maximum(m, jnp.max(s, axis=0, keepdims=True))
            alpha = jnp.exp(m - m_new)
            p = jnp.exp(s - m_new)
            l_ref[...] = alpha * l_ref[...] + jnp.sum(p, axis=0, keepdims=True)
            m_ref[...] = m_new
            probs.append((alpha, p.astype(bf16)))
        for (h, c, (_, _, acc_ref)), (alpha, p16) in zip(subs, probs):
            acc_ref[...] = alpha * acc_ref[...] + _dot(vtj[h * hd:(h + 1) * hd, :], p16)

    scores(0, sa_ref)

    def body(p, carry):
        j = 2 * p
        scores(j + 1, sb_ref)
        softmax_pv(j, sa_ref, False)
        scores(j + 2, sa_ref)
        softmax_pv(j + 1, sb_ref, False)
        return carry

    lax.fori_loop(0, i // 2, body, 0)

    @pl.when(i % 2 == 0)
    def _():
        softmax_pv(i, sa_ref, True)

    @pl.when(i % 2 == 1)
    def _():
        scores(i, sb_ref)
        softmax_pv(i - 1, sa_ref, False)
        softmax_pv(i, sb_ref, True)

    o_t = jnp.concatenate(
        [jnp.concatenate([subs[h * halves + c][2][2][...] / subs[h * halves + c][2][1][...] for c in range(halves)], axis=1)
         for h in range(2)], axis=0)
    o_ref[...] = o_t.T.astype(o_ref.dtype)


def _moba(qa, ka, vat, kmean):
    bsz, seq, width = qa.shape
    nblk = seq // MOBA_BLOCK
    pairs = width // LANES
    t = MOBA_BLOCK
    return pl.pallas_call(
        _moba_kernel,
        grid=(bsz, pairs, nblk),
        in_specs=[
            pl.BlockSpec((None, t, LANES), lambda b, p, i: (b, i, p)),
            pl.BlockSpec((None, seq, LANES), lambda b, p, i: (b, 0, p)),
            pl.BlockSpec((None, nblk, LANES, t), lambda b, p, i: (b, 0, p, 0)),
            pl.BlockSpec((None, nblk, LANES), lambda b, p, i: (b, 0, p)),
        ],
        out_specs=pl.BlockSpec((None, t, LANES), lambda b, p, i: (b, i, p)),
        out_shape=jax.ShapeDtypeStruct((bsz, seq, width), bf16),
        scratch_shapes=[pltpu.VMEM((2, nblk, t), f32), pltpu.VMEM((2, t, LANES), bf16),
                        pltpu.VMEM((2, t, t), f32), pltpu.VMEM((2, t, t), f32)] + 2 * (t // LANES) * [
            pltpu.VMEM((1, LANES), f32), pltpu.VMEM((1, LANES), f32), pltpu.VMEM((HEAD_DIM, LANES), f32)],
        compiler_params=_params(("parallel", "parallel", "arbitrary")),
        name="moba_attention",
    )(qa, ka, vat, kmean)


DIL_TILE = 2048


def _dilated_kernel(q_ref, kp_ref, ko_ref, vp_ref, vo_ref, o_ref, kbuf, vbuf, og_ref, lg_ref):
    t = pl.program_id(1)
    tile = q_ref.shape[0]
    w = BAND_BLOCK
    kbuf[0:tile] = kp_ref[...]
    kbuf[tile:2 * tile] = ko_ref[...]
    vbuf[0:tile] = vp_ref[...]
    vbuf[tile:2 * tile] = vo_ref[...]
    qk_h0, v_h0 = _head_masks()
    qi = lax.broadcasted_iota(i32, (w, 2 * w), 0)
    kc = lax.broadcasted_iota(i32, (w, 2 * w), 1)

    def unit_group(g, dil, units):
        span = w * dil
        work = []
        for qs, first in units:
            if dil == 1:
                rows_q, rows_k = pl.ds(qs, w), pl.ds(tile - span + qs, 2 * w)
            else:
                rows_q, rows_k = pl.ds(qs, w, stride=dil), pl.ds(tile - span + qs, 2 * w, stride=dil)
            q = q_ref[rows_q, :]
            kk = kbuf[rows_k, :].astype(bf16)
            allowed = (kc >= jnp.maximum(qi, jnp.where(first, w, 0))) & (kc <= qi + w)
            scores = [_dot_nt(jnp.where(qk_h0 if h == 0 else jnp.logical_not(qk_h0), q, 0.0).astype(bf16), kk)
                      for h in range(2)]
            work.append((rows_q, rows_k, allowed, scores))
        soft = []
        for rows_q, rows_k, allowed, scores in work:
            per_head = []
            for s in scores:
                s = jnp.where(allowed, s, NEG)
                m = jnp.max(s, axis=-1, keepdims=True)
                e = jnp.exp(s - m)
                l = jnp.sum(e, axis=-1, keepdims=True)
                per_head.append((e.astype(bf16), l, m + jnp.log(l)))
            soft.append(per_head)
        for (rows_q, rows_k, _, _), per_head in zip(work, soft):
            vv = vbuf[rows_k, :].astype(bf16)
            outs = [_dot(e16, vv) / l for e16, l, _ in per_head]
            og_ref[g, rows_q, :] = jnp.where(v_h0, outs[0], outs[1])
            lg_ref[g, rows_q, :] = jnp.where(v_h0, per_head[0][2], per_head[1][2])

    first_tile = t == 0
    group = 4
    blocks = tile // w
    for g, (window, dil) in enumerate(DILATED_PAIRS):
        assert window // dil == w and tile % (w * dil) == 0 and blocks % group == 0

        def body(it, carry, g=g, dil=dil):
            units = []
            for uu in range(group):
                if dil == 1:
                    units.append((pl.multiple_of((group * it + uu) * w, w), first_tile & (group * it + uu == 0)))
                elif dil == group:
                    units.append((pl.multiple_of(it * w * dil, w * dil) + uu, first_tile & (it == 0)))
                else:
                    assert tile == w * dil
                    units.append((group * it + uu, first_tile))
            unit_group(g, dil, units)
            return carry

        lax.fori_loop(0, blocks // group, body, 0)

    l0, l1, l2 = lg_ref[0], lg_ref[1], lg_ref[2]
    mx = jnp.maximum(jnp.maximum(l0, l1), l2)
    e0, e1, e2 = jnp.exp(l0 - mx), jnp.exp(l1 - mx), jnp.exp(l2 - mx)
    den = e0 + e1 + e2
    out = (e0 / den) * og_ref[0] + (e1 / den) * og_ref[1] + (e2 / den) * og_ref[2]
    o_ref[...] = out.astype(o_ref.dtype)


def _dilated(qb, kb, vb):
    bsz, seq, width = qb.shape
    tile = DIL_TILE
    assert seq % tile == 0 and tile == BAND_BLOCK * max(d for _, d in DILATED_PAIRS)
    pairs = width // LANES
    own = lambda b, t, p: (b, t, p)
    prev = lambda b, t, p: (b, jnp.maximum(t - 1, 0), p)
    spec = lambda m: pl.BlockSpec((None, tile, LANES), m)
    return pl.pallas_call(
        _dilated_kernel,
        grid=(bsz, seq // tile, pairs),
        in_specs=[spec(own), spec(prev), spec(own), spec(prev), spec(own)],
        out_specs=spec(own),
        out_shape=jax.ShapeDtypeStruct((bsz, seq, width), bf16),
        scratch_shapes=[
            pltpu.VMEM((2 * tile, LANES), f32),
            pltpu.VMEM((2 * tile, LANES), f32),
            pltpu.VMEM((len(DILATED_PAIRS), tile, LANES), f32),
            pltpu.VMEM((len(DILATED_PAIRS), tile, LANES), f32),
        ],
        compiler_params=_params(("parallel", "parallel", "parallel")),
        name="dilated_attention",
    )(qb, kb, kb, vb, vb)


def _out_proj_ln_kernel(*refs, n_parts, alpha):
    o_refs = refs[:n_parts]
    w_refs = refs[n_parts:2 * n_parts]
    x_ref, g_ref, b_ref, y_ref = refs[2 * n_parts:]
    mix = _dot(o_refs[0][...], w_refs[0][...])
    for o_r, w_r in zip(o_refs[1:], w_refs[1:]):
        mix = mix + _dot(o_r[...], w_r[...])
    y_ref[...] = _layer_norm(alpha * x_ref[...] + mix, g_ref[...], b_ref[...])


def _out_proj_ln(parts, weights, x2d, g, b, alpha):
    n, d = x2d.shape
    tm = ROW_TILE
    row = lambda i: (i, 0)
    const = lambda i: (0, 0)
    in_specs = ([pl.BlockSpec((tm, p.shape[1]), row) for p in parts]
                + [pl.BlockSpec(w.shape, const) for w in weights]
                + [pl.BlockSpec((tm, d), row), pl.BlockSpec((1, d), const), pl.BlockSpec((1, d), const)])
    return pl.pallas_call(
        functools.partial(_out_proj_ln_kernel, n_parts=len(parts), alpha=alpha),
        grid=(n // tm,),
        in_specs=in_specs,
        out_specs=pl.BlockSpec((tm, d), row),
        out_shape=jax.ShapeDtypeStruct((n, d), f32),
        compiler_params=_params(("parallel",)),
        name="out_proj_ln",
    )(*parts, *weights, x2d, g.reshape(1, d), b.reshape(1, d))


def _router_kernel(x_ref, wh_ref, wl_ref, bias_ref, idx_ref, gate_ref, cnt_ref, run_ref):
    tm = x_ref.shape[0]
    ne = wh_ref.shape[0]
    eg = EXPERTS_PER_GROUP

    @pl.when(pl.program_id(0) == 0)
    def _():
        run_ref[...] = jnp.zeros(run_ref.shape, f32)

    x_s = _split(x_ref[...])
    wh, wl = wh_ref[...], wl_ref[...]
    logits = _dot_nt(wh, x_s[0]) + (_dot_nt(wh, x_s[1]) + _dot_nt(wl, x_s[0]))
    scores = _sigmoid(logits)
    sel = (scores + bias_ref[...]).reshape(N_GROUPS, eg, tm)
    loc = lax.broadcasted_iota(i32, (N_GROUPS, eg, tm), 1)
    m1 = jnp.max(sel, axis=1, keepdims=True)
    i1 = jnp.min(jnp.where(sel == m1, loc, eg), axis=1, keepdims=True)
    rest = jnp.where(loc == i1, -jnp.inf, sel)
    m2 = jnp.max(rest, axis=1, keepdims=True)
    i2 = jnp.min(jnp.where(rest == m2, loc, eg), axis=1, keepdims=True)
    gs = m1 + m2
    best, grp, l1, l2 = gs[0], jnp.zeros((1, tm), i32), i1[0], i2[0]
    for g in range(1, N_GROUPS):
        better = gs[g] > best
        best = jnp.where(better, gs[g], best)
        grp = jnp.where(better, g, grp)
        l1 = jnp.where(better, i1[g], l1)
        l2 = jnp.where(better, i2[g], l2)
    e1 = grp * eg + l1
    e2 = grp * eg + l2
    row = lax.broadcasted_iota(i32, (ne, tm), 0)
    oh1 = row == e1
    oh2 = row == e2
    g1 = jnp.sum(jnp.where(oh1, scores, 0.0), axis=0, keepdims=True)
    g2 = jnp.sum(jnp.where(oh2, scores, 0.0), axis=0, keepdims=True)
    gsum = g1 + g2
    both = jnp.where(oh1 | oh2, 1.0, 0.0)
    r_i = lax.broadcasted_iota(i32, (tm, tm), 0)
    c_i = lax.broadcasted_iota(i32, (tm, tm), 1)
    tri = jnp.where(r_i < c_i, 1.0, 0.0).astype(bf16)
    before = _dot(both.astype(bf16), tri) + run_ref[...]
    r1 = jnp.sum(jnp.where(oh1, before, 0.0), axis=0, keepdims=True)
    r2 = jnp.sum(jnp.where(oh2, before, 0.0), axis=0, keepdims=True)
    run_ref[...] = run_ref[...] + jnp.sum(both, axis=1, keepdims=True)
    pad = SUBLANES - 2 * TOP_K
    idx_ref[...] = jnp.concatenate([e1.astype(f32), e2.astype(f32), r1, r2] + pad * [r2], axis=0).astype(i32)
    gate_ref[...] = jnp.concatenate([g1 / gsum, g2 / gsum] + (SUBLANES - TOP_K) * [g2], axis=0)
    cnt_ref[...] = run_ref[...].astype(i32)


def _router(x2d, router_w, router_bias):
    n, d = x2d.shape
    tm = ROW_TILE
    ne = router_w.shape[1]
    wt = router_w.T
    wt_hi = wt.astype(bf16)
    wt_lo = (wt - wt_hi.astype(f32)).astype(bf16)
    return pl.pallas_call(
        _router_kernel,
        grid=(n // tm,),
        in_specs=[
            pl.BlockSpec((tm, d), lambda i: (i, 0)),
            pl.BlockSpec((ne, d), lambda i: (0, 0)),
            pl.BlockSpec((ne, d), lambda i: (0, 0)),
            pl.BlockSpec((ne, 1), lambda i: (0, 0)),
        ],
        out_specs=(
            pl.BlockSpec((SUBLANES, tm), lambda i: (0, i)),
            pl.BlockSpec((SUBLANES, tm), lambda i: (0, i)),
            pl.BlockSpec((ne, 1), lambda i: (0, 0)),
        ),
        out_shape=(
            jax.ShapeDtypeStruct((SUBLANES, n), i32),
            jax.ShapeDtypeStruct((SUBLANES, n), f32),
            jax.ShapeDtypeStruct((ne, 1), i32),
        ),
        scratch_shapes=[pltpu.VMEM((ne, 1), f32)],
        compiler_params=_params(("arbitrary",)),
        name="moe_router",
    )(x2d, wt_hi, wt_lo, router_bias.reshape(ne, 1))


def _row_dma_loops(tm, copy):
    def start(g, c):
        base = pl.multiple_of(g * SUBLANES, SUBLANES)
        for r in range(SUBLANES):
            for k in range(TOP_K):
                copy(base + r, k).start()
        return c

    def wait(g, c):
        base = pl.multiple_of(g * SUBLANES, SUBLANES)
        for r in range(SUBLANES):
            for k in range(TOP_K):
                copy(base + r, k).wait()
        return c

    lax.fori_loop(0, tm // SUBLANES, start, 0)
    lax.fori_loop(0, tm // SUBLANES, wait, 0)


def _to_tile_major(dst_ref, value):
    rows = value.shape[0]
    for c in range(value.shape[1] // LANES):
        dst_ref[pl.ds(c, rows, stride=SUBLANES), :] = value[:, c * LANES:(c + 1) * LANES]


def _from_tile_major(src_ref, rows, chunks):
    return jnp.concatenate([src_ref[pl.ds(c, rows, stride=SUBLANES), :] for c in range(chunks)], axis=1)


def _dispatch_kernel(d0_ref, d1_ref, x_ref, xs_in, xs_hbm, x3, sem):
    del xs_in
    tm = x_ref.shape[0]
    dests = (d0_ref, d1_ref)
    _to_tile_major(x3, x_ref[...])

    def copy(t, k):
        return pltpu.make_async_copy(x3.at[pl.ds(t * SUBLANES, SUBLANES)],
                                     xs_hbm.at[pl.ds(pl.multiple_of(dests[k][t], SUBLANES), SUBLANES)], sem)

    _row_dma_loops(tm, copy)


def _dispatch(x2d, dest8, cap):
    n, d = x2d.shape
    assert d == SUBLANES * LANES
    tm = ROW_TILE
    xs0 = jnp.zeros((cap * SUBLANES, LANES), x2d.dtype)
    return pl.pallas_call(
        _dispatch_kernel,
        grid=(n // tm,),
        in_specs=[
            pl.BlockSpec((tm,), lambda i: (i,), memory_space=pltpu.SMEM),
            pl.BlockSpec((tm,), lambda i: (i,), memory_space=pltpu.SMEM),
            pl.BlockSpec((tm, d), lambda i: (i, 0)),
            pl.BlockSpec(memory_space=pl.ANY),
        ],
        out_specs=pl.BlockSpec(memory_space=pl.ANY),
        out_shape=jax.ShapeDtypeStruct((cap * SUBLANES, LANES), x2d.dtype),
        scratch_shapes=[pltpu.VMEM((tm * SUBLANES, LANES), x2d.dtype), pltpu.SemaphoreType.DMA(())],
        input_output_aliases={3: 0},
        compiler_params=_params(("arbitrary",), has_side_effects=True, disable_bounds_checks=True),
        name="moe_dispatch",
    )(dest8[0], dest8[1], x2d, xs0)


def _expert_kernel(be_ref, nused_ref, xs_ref, w1_ref, w3_ref, w2_ref, ys_ref, w1b, w3b, w2b):
    b = pl.program_id(0)
    changed = jnp.logical_or(b == 0, be_ref[b] != be_ref[jnp.maximum(b - 1, 0)])

    @pl.when(changed)
    def _():
        w1b[...] = w1_ref[...].astype(bf16)
        w3b[...] = w3_ref[...].astype(bf16)
        w2b[...] = w2_ref[...].astype(bf16)

    @pl.when(b < nused_ref[0])
    def _():
        xb = _from_tile_major(xs_ref, MOE_BLOCK, w1b.shape[0] // LANES).astype(bf16)
        h1 = _dot(xb, w1b[...])
        h3 = _dot(xb, w3b[...])
        hb = (h1 * _sigmoid(h1)) * h3
        _to_tile_major(ys_ref, _dot(hb.astype(bf16), w2b[...]))

    @pl.when(b >= nused_ref[0])
    def _():
        ys_ref[...] = jnp.zeros(ys_ref.shape, f32)


def _expert_ffn(xs, block_expert, n_used, w1, w3, w2, layer):
    d, dff = w1.shape[-2:]
    cap = xs.shape[0] // SUBLANES
    nblocks = cap // MOE_BLOCK
    tile_rows = MOE_BLOCK * SUBLANES
    grid_spec = pltpu.PrefetchScalarGridSpec(
        num_scalar_prefetch=2,
        grid=(nblocks,),
        in_specs=[
            pl.BlockSpec((tile_rows, LANES), lambda b, be, nu: (b, 0)),
            pl.BlockSpec((None, None, d, dff), lambda b, be, nu: (layer, be[b], 0, 0)),
            pl.BlockSpec((None, None, d, dff), lambda b, be, nu: (layer, be[b], 0, 0)),
            pl.BlockSpec((None, None, dff, d), lambda b, be, nu: (layer, be[b], 0, 0)),
        ],
        out_specs=pl.BlockSpec((tile_rows, LANES), lambda b, be, nu: (b, 0)),
        scratch_shapes=[pltpu.VMEM((d, dff), bf16), pltpu.VMEM((d, dff), bf16), pltpu.VMEM((dff, d), bf16)],
    )
    return pl.pallas_call(
        _expert_kernel,
        grid_spec=grid_spec,
        out_shape=jax.ShapeDtypeStruct((cap * SUBLANES, LANES), f32),
        compiler_params=_params(("arbitrary",)),
        name="moe_expert_ffn",
    )(block_expert, n_used, xs, w1, w3, w2)


def _combine_kernel(d0_ref, d1_ref, x_ref, gate_ref, g_ref, b_ref, ys_hbm, y_ref, buf, sem, *, alpha):
    tm = x_ref.shape[0]
    dests = (d0_ref, d1_ref)

    def copy(t, k):
        return pltpu.make_async_copy(ys_hbm.at[pl.ds(pl.multiple_of(dests[k][t], SUBLANES), SUBLANES)],
                                     buf.at[k, pl.ds(t * SUBLANES, SUBLANES)], sem)

    _row_dma_loops(tm, copy)
    gates = gate_ref[...].T
    chunks = x_ref.shape[1] // LANES
    ffn = (_from_tile_major(buf.at[0], tm, chunks) * gates[:, 0:1]
           + _from_tile_major(buf.at[1], tm, chunks) * gates[:, 1:2])
    y_ref[...] = _layer_norm(alpha * x_ref[...] + ffn, g_ref[...], b_ref[...])


def _combine_ln(x2d, ys, dest, gates, g, b, alpha):
    n, d = x2d.shape
    tm = ROW_TILE
    return pl.pallas_call(
        functools.partial(_combine_kernel, alpha=alpha),
        grid=(n // tm,),
        in_specs=[
            pl.BlockSpec((tm,), lambda i: (i,), memory_space=pltpu.SMEM),
            pl.BlockSpec((tm,), lambda i: (i,), memory_space=pltpu.SMEM),
            pl.BlockSpec((tm, d), lambda i: (i, 0)),
            pl.BlockSpec((SUBLANES, tm), lambda i: (0, i)),
            pl.BlockSpec((1, d), lambda i: (0, 0)),
            pl.BlockSpec((1, d), lambda i: (0, 0)),
            pl.BlockSpec(memory_space=pl.ANY),
        ],
        out_specs=pl.BlockSpec((tm, d), lambda i: (i, 0)),
        out_shape=jax.ShapeDtypeStruct((n, d), f32),
        scratch_shapes=[pltpu.VMEM((TOP_K, tm * SUBLANES, LANES), f32), pltpu.SemaphoreType.DMA(())],
        compiler_params=_params(("arbitrary",), disable_bounds_checks=True),
        name="moe_combine_ln",
    )(dest[0], dest[1], x2d, gates, g.reshape(1, d), b.reshape(1, d), ys)


def _moe_ln(x2d, router_w, router_bias, w1, w3, w2, layer, g, b, alpha):
    n, d = x2d.shape
    idx, gates, counts = _router(x2d, router_w, router_bias)
    counts = counts.reshape(-1)
    padded = (counts + MOE_BLOCK - 1) // MOE_BLOCK * MOE_BLOCK
    pad_end = jnp.cumsum(padded)
    pad_start = pad_end - padded
    nblocks = n * TOP_K // MOE_BLOCK + N_EXPERTS
    cap = nblocks * MOE_BLOCK
    block_start = jnp.arange(nblocks, dtype=i32) * MOE_BLOCK
    block_expert = jnp.minimum(jnp.sum(pad_end[None, :] <= block_start[:, None], axis=1), N_EXPERTS - 1).astype(i32)
    n_used = (pad_end[-1:] // MOE_BLOCK).astype(i32)
    experts = jnp.arange(N_EXPERTS, dtype=i32)[:, None]
    dest = [((jnp.sum(jnp.where(idx[k][None, :] == experts, pad_start[:, None], 0), axis=0) + idx[TOP_K + k])
             * SUBLANES).astype(i32) for k in range(TOP_K)]
    xs = _dispatch(x2d, dest, cap)
    ys = _expert_ffn(xs, block_expert, n_used, w1, w3, w2, layer)
    return _combine_ln(x2d, ys, dest, gates, g, b, alpha)


def _gdn_proj_kernel(x_ref, wc_ref, wz_ref, wba_ref, cw_ref, alog_ref, dt_ref,
                     qkv_ref, z_ref, beta_ref, gc_ref, buf, *, tiles_per_seq):
    i = pl.program_id(0)
    tm = x_ref.shape[0]
    halo = 8
    xb = x_ref[...].astype(bf16)
    qk_cols = 2 * GDN_QK_HEADS * GDN_HEAD_DIM
    conv_ch = wc_ref.shape[1]

    @pl.when(i % tiles_per_seq == 0)
    def _():
        buf[0:halo, :] = jnp.zeros((halo, conv_ch), f32)

    buf[halo:halo + tm, :] = _dot(xb, wc_ref[...])
    q_scale = GDN_HEAD_DIM ** -0.5
    for c in range(conv_ch // LANES):
        cols = slice(c * LANES, (c + 1) * LANES)
        y = jnp.zeros((tm, LANES), f32)
        for j in range(GDN_CONV):
            shift = GDN_CONV - 1 - j
            y = y + buf[halo - shift:halo - shift + tm, cols] * cw_ref[j:j + 1, cols]
        y = y * _sigmoid(y)
        if c * LANES < qk_cols:
            y = y * lax.rsqrt(jnp.sum(y * y, axis=-1, keepdims=True) + NORM_EPS)
            if c * LANES < qk_cols // 2:
                y = y * q_scale
        qkv_ref[:, cols] = y
    buf[0:halo, :] = buf[tm:tm + halo, :]

    z_ref[...] = _dot(xb, wz_ref[...])
    ba = _dot_nt(wba_ref[...], xb)
    hv = GDN_V_HEADS
    beta_ref[...] = _sigmoid(ba[:hv])
    a = ba[hv:] + dt_ref[...]
    softplus = jnp.maximum(a, 0.0) + jnp.log(1.0 + jnp.exp(-jnp.abs(a)))
    g = -jnp.exp(alog_ref[...]) * softplus
    r_i = lax.broadcasted_iota(i32, (tm, tm), 0)
    c_i = lax.broadcasted_iota(i32, (tm, tm), 1)
    same_chunk = (r_i // GDN_CHUNK) == (c_i // GDN_CHUNK)
    cum = jnp.where(same_chunk & (r_i <= c_i), 1.0, 0.0)
    gc_ref[...] = _dot_hi(g, cum)


def _gdn_proj(x2d, w_conv, w_z, w_ba_t, conv_w, a_log, dt_bias, seq):
    n, d = x2d.shape
    tm = ROW_TILE // 2
    conv_ch = w_conv.shape[1]
    vdim = w_z.shape[1]
    hv = GDN_V_HEADS
    row = lambda i: (i, 0)
    const = lambda i: (0, 0)
    return pl.pallas_call(
        functools.partial(_gdn_proj_kernel, tiles_per_seq=seq // tm),
        grid=(n // tm,),
        in_specs=[
            pl.BlockSpec((tm, d), row),
            pl.BlockSpec(w_conv.shape, const, pipeline_mode=pl.Buffered(1)),
            pl.BlockSpec(w_z.shape, const, pipeline_mode=pl.Buffered(1)),
            pl.BlockSpec(w_ba_t.shape, const),
            pl.BlockSpec(conv_w.shape, const),
            pl.BlockSpec((hv, 1), const),
            pl.BlockSpec((hv, 1), const),
        ],
        out_specs=(
            pl.BlockSpec((tm, conv_ch), row),
            pl.BlockSpec((tm, vdim), row),
            pl.BlockSpec((hv, tm), lambda i: (0, i)),
            pl.BlockSpec((hv, tm), lambda i: (0, i)),
        ),
        out_shape=(
            jax.ShapeDtypeStruct((n, conv_ch), f32),
            jax.ShapeDtypeStruct((n, vdim), f32),
            jax.ShapeDtypeStruct((hv, n), f32),
            jax.ShapeDtypeStruct((hv, n), f32),
        ),
        scratch_shapes=[pltpu.VMEM((tm + 16, conv_ch), f32)],
        compiler_params=_params(("arbitrary",)),
        name="gdn_proj",
    )(x2d, w_conv, w_z, w_ba_t, conv_w, a_log.reshape(hv, 1), dt_bias.reshape(hv, 1))


def _gdn_delta_kernel(q_ref, k_ref, v_ref, z_ref, betar_ref, gcr_ref, ng_ref, o_ref, state):
    ts = q_ref.shape[0]
    c = GDN_CHUNK

    @pl.when(pl.program_id(2) == 0)
    def _():
        state[...] = jnp.zeros(state.shape, f32)

    hb = state.shape[0]
    hd = GDN_HEAD_DIM
    sub = GDN_SOLVE
    nch = sub // c
    r_i = lax.broadcasted_iota(i32, (sub, sub), 0)
    c_i = lax.broadcasted_iota(i32, (sub, sub), 1)
    same = (r_i // c) == (c_i // c)
    lower = same & (r_i >= c_i)
    strict = same & (r_i > c_i)
    inst = [(st, h) for st in range(ts // sub) for h in range(hb)]
    hs = range(len(inst))
    rows_of = [slice(st * sub, (st + 1) * sub) for st, _ in inst]
    q = [q_ref[rows_of[i], :] for i in hs]
    k = [k_ref[rows_of[i], :] for i in hs]
    k16 = [k[i].astype(bf16) for i in hs]
    qk = [_dot_nt(q[i].astype(bf16), k16[i]) for i in hs]

    rows8 = jnp.concatenate([betar_ref[h] for h in range(hb)] + [gcr_ref[h] for h in range(hb)]
                            + (SUBLANES - 2 * hb) * [gcr_ref[0]], axis=0)
    cols8 = rows8.T
    beta = [cols8[rows_of[i], h:h + 1] for i, (_, h) in enumerate(inst)]
    gcc = [cols8[rows_of[i], hb + h:hb + h + 1] for i, (_, h) in enumerate(inst)]
    gcr = [gcr_ref[h][:, rows_of[i]] for i, (_, h) in enumerate(inst)]
    decay = [jnp.where(lower, jnp.exp(jnp.where(lower, gcc[h] - gcr[h], 0.0)), 0.0) for h in hs]
    egc = [jnp.exp(gcc[h]) for h in hs]
    kb = [k[h] * beta[h] for h in hs]
    kk = [_dot_nt(kb[h].astype(bf16), k16[h]) for h in hs]
    l16 = [jnp.where(strict, kk[h] * decay[h], 0.0).astype(bf16) for h in hs]
    rhs = [jnp.concatenate([v_ref[rows_of[i], h * hd:(h + 1) * hd] * beta[i], kb[i] * egc[i]], axis=-1)
           for i, (_, h) in enumerate(inst)]
    pows = [l16]
    span = 2
    while span < c:
        pows.append([_dot(pows[-1][h], pows[-1][h]).astype(bf16) for h in hs])
        span *= 2
    sol = [rhs[h] - _dot(pows[0][h], rhs[h].astype(bf16)) for h in hs]
    for p in pows[1:]:
        sol = [sol[h] + _dot(p[h], sol[h].astype(bf16)) for h in hs]
    sol16 = [sol[h].astype(bf16) for h in hs]
    auw = [_dot((qk[h] * decay[h]).astype(bf16), sol16[h]) for h in hs]
    parts = []
    for i in hs:
        o0 = auw[i][:, :hd]
        qe16 = (q[i] * egc[i] - auw[i][:, hd:]).astype(bf16)
        ab, cd = [], []
        for n in range(nch):
            rows = slice(n * c, (n + 1) * c)
            g_last = gcr[i][:, (n + 1) * c - 1:(n + 1) * c]
            kd16 = (k[i][rows] * jnp.exp(g_last - gcc[i][rows])).astype(bf16)
            ab.append(_dot_tn(kd16, sol16[i][rows]))
            cd.append(jnp.exp(g_last))
        parts.append((o0, qe16, ab, cd))

    outs = [[] for _ in range(hb)]
    s_cur = [state[h] for h in range(hb)]
    for st in range(ts // sub):
        for n in range(nch):
            rows = slice(n * c, (n + 1) * c)
            for h in range(hb):
                o0, qe16, ab, cd = parts[st * hb + h]
                s16 = s_cur[h].astype(bf16)
                outs[h].append(_dot(qe16[rows], s16) + o0[rows])
                s_cur[h] = s_cur[h] * cd[n] - _dot(ab[n][:, hd:].astype(bf16), s16) + ab[n][:, :hd]
    for h in range(hb):
        state[h] = s_cur[h]
        o = jnp.concatenate(outs[h], axis=0)
        z = z_ref[:, h * hd:(h + 1) * hd]
        o = o * lax.rsqrt(jnp.mean(o * o, axis=-1, keepdims=True) + NORM_EPS) * ng_ref[...] * (z * _sigmoid(z))
        o_ref[:, h * hd:(h + 1) * hd] = o.astype(o_ref.dtype)


def _gdn_delta(qkv, z, beta_row, gc_row, norm_g, ts):
    bsz, seq, _ = qkv.shape
    hd = GDN_HEAD_DIM
    rep = GDN_V_HEADS // GDN_QK_HEADS
    tiles = seq // ts
    row_spec = pl.BlockSpec((rep, None, 1, ts), lambda b, g, s: (g, b * tiles + s, 0, 0))
    v_off = 2 * GDN_QK_HEADS // rep
    return pl.pallas_call(
        _gdn_delta_kernel,
        grid=(bsz, GDN_QK_HEADS, seq // ts),
        in_specs=[
            pl.BlockSpec((None, ts, hd), lambda b, g, s: (b, s, g)),
            pl.BlockSpec((None, ts, hd), lambda b, g, s: (b, s, GDN_QK_HEADS + g)),
            pl.BlockSpec((None, ts, rep * hd), lambda b, g, s: (b, s, v_off + g)),
            pl.BlockSpec((None, ts, rep * hd), lambda b, g, s: (b, s, g)),
            row_spec,
            row_spec,
            pl.BlockSpec((1, hd), lambda b, g, s: (0, 0)),
        ],
        out_specs=pl.BlockSpec((None, ts, rep * hd), lambda b, g, s: (b, s, g)),
        out_shape=jax.ShapeDtypeStruct((bsz, seq, GDN_V_HEADS * hd), bf16),
        scratch_shapes=[pltpu.VMEM((rep, hd, hd), f32)],
        compiler_params=_params(("parallel", "parallel", "arbitrary")),
        name="gdn_delta",
    )(qkv, qkv, qkv, z, beta_row, gc_row, norm_g.reshape(1, hd))


def _rope_slab_tables(seq):
    half = HEAD_DIM // 2
    inv = ROPE_THETA ** (-jnp.arange(0, HEAD_DIM, 2, dtype=f32) / HEAD_DIM)
    ang = jnp.arange(seq, dtype=f32)[:, None] * inv[None, :]
    cos, sin = jnp.cos(ang), jnp.sin(ang)
    cos_t = jnp.concatenate([cos] * (LANES // half), axis=1)
    sin_t = jnp.concatenate([-sin, -sin, sin, sin], axis=1)
    return cos_t, sin_t


def _attention_layer(x2d, bsz, seq, w_in, w_out, cos_t, sin_t, g, b, alpha):
    n, d = x2d.shape
    w_perm = w_in[:, _ab_col_perm()].astype(bf16)
    qa, ka, vat, qb, kb, vb, kmean = _ab_proj(x2d, w_perm, cos_t, sin_t, seq)
    sh = lambda t: t.reshape(bsz, seq, t.shape[-1])
    nblk = seq // MOBA_BLOCK
    o_a = _moba(sh(qa), sh(ka), vat.reshape(bsz, nblk, -1, MOBA_BLOCK), kmean.reshape(bsz, nblk, -1))
    o_b = _dilated(sh(qb), sh(kb), sh(vb))
    na = N_HEADS_A * HEAD_DIM
    w_o = w_out.astype(bf16)
    return _out_proj_ln([o_a.reshape(n, -1), o_b.reshape(n, -1)], [w_o[:na], w_o[na:]], x2d, g, b, alpha)


def _gdn_layer(x2d, bsz, seq, w_in, conv_w, a_log, dt_bias, norm_g, w_out, g, b, alpha):
    n, d = x2d.shape
    conv_ch = conv_w.shape[1]
    vdim = GDN_V_HEADS * GDN_HEAD_DIM
    w16 = w_in.astype(bf16)
    qkv, z, beta, gc = _gdn_proj(x2d, w16[:, :conv_ch], w16[:, conv_ch:conv_ch + vdim], w16[:, conv_ch + vdim:].T,
                                 conv_w, a_log, dt_bias, seq)
    rows = lambda t: t.reshape(GDN_V_HEADS, n // GDN_TILE, 1, GDN_TILE)
    o = _gdn_delta(qkv.reshape(bsz, seq, -1), z.reshape(bsz, seq, -1), rows(beta), rows(gc), norm_g, GDN_TILE)
    return _out_proj_ln([o.reshape(n, -1)], [w_out.astype(bf16)], x2d, g, b, alpha)


def kernel(x, ab_w_in, ab_w_out, gdn_w_in, gdn_conv_w, gdn_a_log, gdn_dt_bias, gdn_norm_g, gdn_w_out, mix_ln_g, mix_ln_b, router_w, router_bias, moe_w1, moe_w3, moe_w2, ffn_ln_g, ffn_ln_b):
    bsz, seq, d = x.shape
    depth = mix_ln_g.shape[0]
    alpha = (2.0 * depth) ** 0.25
    cos_t, sin_t = _rope_slab_tables(seq)
    h = x.reshape(bsz * seq, d)
    for layer in range(depth):
        j = layer // 2
        if layer % 2 == 0:
            h = _attention_layer(h, bsz, seq, ab_w_in[j], ab_w_out[j], cos_t, sin_t,
                                 mix_ln_g[layer], mix_ln_b[layer], alpha)
        else:
            h = _gdn_layer(h, bsz, seq, gdn_w_in[j], gdn_conv_w[j], gdn_a_log[j], gdn_dt_bias[j], gdn_norm_g[j],
                           gdn_w_out[j], mix_ln_g[layer], mix_ln_b[layer], alpha)
        h = _moe_ln(h, router_w, router_bias, moe_w1, moe_w3, moe_w2, layer, ffn_ln_g[layer], ffn_ln_b[layer], alpha)
    return h.reshape(bsz, seq, d)
```

```python
import functools

import numpy as np
import jax
import jax.numpy as jnp
from jax import lax
from jax.experimental import pallas as pl
from jax.experimental.pallas import tpu as pltpu

f32 = jnp.float32
bf16 = jnp.bfloat16
i32 = jnp.int32

HEAD_DIM = 64
N_HEADS_A = 8
N_HEADS_B = 8
AB_HEADS = N_HEADS_A + N_HEADS_B
MOBA_BLOCK = 256
MOBA_TOPK = 3
DILATED_PAIRS = ((128, 1), (512, 4), (2048, 16))
BAND_BLOCK = 128
ROPE_THETA = 10000.0

GDN_QK_HEADS = 8
GDN_V_HEADS = 16
GDN_HEAD_DIM = 128
GDN_CONV = 4
GDN_CHUNK = 64

N_EXPERTS = 32
N_GROUPS = 4
EXPERTS_PER_GROUP = N_EXPERTS // N_GROUPS
TOP_K = 2
MOE_BLOCK = 256

LN_EPS = 1e-5
NORM_EPS = 1e-6
NEG = -1e30

LANES = 128
SUBLANES = 8
ROW_TILE = 512
GDN_SOLVE = 256
GDN_TILE = 1024
VMEM_LIMIT = 56 * 1024 * 1024


def _params(semantics, **kw):
    return pltpu.CompilerParams(dimension_semantics=semantics, vmem_limit_bytes=VMEM_LIMIT, **kw)


def _dot(a, b):
    return jnp.dot(a, b, preferred_element_type=f32)


def _dot_nt(a, b):
    return lax.dot_general(a, b, (((1,), (1,)), ((), ())), preferred_element_type=f32)


def _dot_tn(a, b):
    return lax.dot_general(a, b, (((0,), (0,)), ((), ())), preferred_element_type=f32)


def _dot_hi(a, b):
    return jnp.dot(a, b, preferred_element_type=f32, precision=lax.Precision.HIGHEST)


def _split(a):
    hi = a.astype(bf16)
    return hi, (a - hi.astype(f32)).astype(bf16)


def _layer_norm(y, g, b):
    mu = jnp.mean(y, axis=-1, keepdims=True)
    d = y - mu
    var = jnp.mean(d * d, axis=-1, keepdims=True)
    return d * lax.rsqrt(var + LN_EPS) * g + b


def _sigmoid(x):
    return 1.0 / (1.0 + jnp.exp(-x))


def _ab_col_perm():
    perm = np.zeros((3, AB_HEADS * HEAD_DIM), np.int32)
    half = HEAD_DIM // 2
    for t in range(3):
        for col in range(AB_HEADS * HEAD_DIM):
            if t == 2:
                perm[t, col] = t * AB_HEADS * HEAD_DIM + col
                continue
            slab, lane = divmod(col, LANES)
            part, within = divmod(lane, half)
            head = 2 * slab + (part % 2)
            dim = within + half * (part // 2)
            perm[t, col] = t * AB_HEADS * HEAD_DIM + head * HEAD_DIM + dim
    return perm.reshape(-1)


def _ab_proj_kernel(x_ref, w_ref, cos_ref, sin_ref, qa_ref, ka_ref, vat_ref, qb_ref, kb_ref, vb_ref, km_ref):
    tm = x_ref.shape[0]
    width = AB_HEADS * HEAD_DIM
    na = N_HEADS_A * HEAD_DIM
    xb = x_ref[...].astype(bf16)
    c = cos_ref[...]
    s = sin_ref[...]
    scale = HEAD_DIM ** -0.5

    def rope(slab):
        return slab * c + pltpu.roll(slab, LANES // 2, 1) * s

    acc = _dot(xb, w_ref[:, 0:width])
    for j in range(width // LANES):
        r = rope(acc[:, j * LANES:(j + 1) * LANES]) * scale
        if j * LANES < na:
            qa_ref[:, j * LANES:(j + 1) * LANES] = r.astype(bf16)
        else:
            qb_ref[:, j * LANES - na:(j + 1) * LANES - na] = r
    acc = _dot(xb, w_ref[:, width:2 * width])
    for j in range(width // LANES):
        r = rope(acc[:, j * LANES:(j + 1) * LANES])
        if j * LANES < na:
            ka_ref[:, j * LANES:(j + 1) * LANES] = r.astype(bf16)
            for blk in range(tm // MOBA_BLOCK):
                km_ref[blk, :, j * LANES:(j + 1) * LANES] = jnp.mean(
                    r[blk * MOBA_BLOCK:(blk + 1) * MOBA_BLOCK], axis=0, keepdims=True)
        else:
            kb_ref[:, j * LANES - na:(j + 1) * LANES - na] = r
    acc = _dot(xb, w_ref[:, 2 * width:3 * width])
    for blk in range(tm // MOBA_BLOCK):
        vat_ref[blk] = acc[blk * MOBA_BLOCK:(blk + 1) * MOBA_BLOCK, :na].T.astype(bf16)
    vb_ref[...] = acc[:, na:]


def _ab_proj(x2d, w_perm, cos_t, sin_t, seq):
    n, d = x2d.shape
    tm = ROW_TILE
    na = N_HEADS_A * HEAD_DIM
    nb = N_HEADS_B * HEAD_DIM
    tiles_per_seq = seq // tm
    row = lambda i: (i, 0)
    out_shape = (
        jax.ShapeDtypeStruct((n, na), bf16), jax.ShapeDtypeStruct((n, na), bf16),
        jax.ShapeDtypeStruct((n // MOBA_BLOCK, na, MOBA_BLOCK), bf16),
        jax.ShapeDtypeStruct((n, nb), f32), jax.ShapeDtypeStruct((n, nb), f32), jax.ShapeDtypeStruct((n, nb), f32),
        jax.ShapeDtypeStruct((n // MOBA_BLOCK, 1, na), f32),
    )
    return pl.pallas_call(
        _ab_proj_kernel,
        grid=(n // tm,),
        in_specs=[
            pl.BlockSpec((tm, d), row),
            pl.BlockSpec(w_perm.shape, lambda i: (0, 0)),
            pl.BlockSpec((tm, LANES), lambda i: (i % tiles_per_seq, 0)),
            pl.BlockSpec((tm, LANES), lambda i: (i % tiles_per_seq, 0)),
        ],
        out_specs=(
            pl.BlockSpec((tm, na), row), pl.BlockSpec((tm, na), row),
            pl.BlockSpec((tm // MOBA_BLOCK, na, MOBA_BLOCK), lambda i: (i, 0, 0)),
            pl.BlockSpec((tm, nb), row), pl.BlockSpec((tm, nb), row), pl.BlockSpec((tm, nb), row),
            pl.BlockSpec((tm // MOBA_BLOCK, 1, na), lambda i: (i, 0, 0)),
        ),
        out_shape=out_shape,
        compiler_params=_params(("parallel",)),
        name="ab_proj",
    )(x2d, w_perm, cos_t, sin_t)


def _head_masks():
    lane = lax.broadcasted_iota(i32, (1, LANES), 1)
    qk_h0 = (lane // (HEAD_DIM // 2)) % 2 == 0
    v_h0 = lane < HEAD_DIM
    return qk_h0, v_h0


def _moba_kernel(q_ref, k_ref, vt_ref, km_ref, o_ref, sel_ref, qh_ref, sa_ref, sb_ref, *state):
    i = pl.program_id(2)
    tq = q_ref.shape[0]
    tk = MOBA_BLOCK
    nblk = km_ref.shape[0]
    hd = HEAD_DIM
    qk_h0, _ = _head_masks()
    q = q_ref[...]
    qh = (jnp.where(qk_h0, q, jnp.zeros_like(q)), jnp.where(qk_h0, jnp.zeros_like(q), q))

    km = km_ref[...]
    km_hi = km.astype(bf16)
    km_lo = (km - km_hi.astype(f32)).astype(bf16)
    blk = lax.broadcasted_iota(i32, (nblk, tq), 0)
    for h in range(2):
        gate = _dot_nt(km_hi, qh[h]) + _dot_nt(km_lo, qh[h])
        gate = jnp.where(blk < i, gate, NEG)
        sel = jnp.zeros((nblk, tq), f32)
        for _ in range(MOBA_TOPK):
            mx = jnp.max(gate, axis=0, keepdims=True)
            idx = jnp.min(jnp.where(gate == mx, blk, nblk), axis=0, keepdims=True)
            sel = jnp.where(blk == jnp.where(idx < i, idx, -1), 1.0, sel)
            gate = jnp.where(blk == idx, -jnp.inf, gate)
        sel_ref[h] = sel

    halves = tq // LANES
    subs = [(h, c, state[3 * (h * halves + c):3 * (h * halves + c) + 3]) for h in range(2) for c in range(halves)]
    for _, _, (m_ref, l_ref, acc_ref) in subs:
        m_ref[...] = jnp.full(m_ref.shape, -jnp.inf, f32)
        l_ref[...] = jnp.zeros(l_ref.shape, f32)
        acc_ref[...] = jnp.zeros(acc_ref.shape, f32)
    for h in range(2):
        qh_ref[h] = qh[h]
    k_io = lax.broadcasted_iota(i32, (tk, LANES), 0)
    q_io = lax.broadcasted_iota(i32, (tk, LANES), 1)

    def scores(j, s_ref):
        kj = k_ref[pl.ds(pl.multiple_of(j * tk, tk), tk), :]
        for h in range(2):
            s_ref[h] = _dot_nt(kj, qh_ref[h])

    def softmax_pv(j, s_ref, own):
        vtj = vt_ref[j]
        sel_rows = None if own else [sel_ref[h, pl.ds(j, 1), :] for h in range(2)]
        probs = []
        for h, c, (m_ref, l_ref, _) in subs:
            cols = slice(c * LANES, (c + 1) * LANES)
            s = s_ref[h, :, cols]
            allowed = (k_io <= q_io + c * LANES) if own else (sel_rows[h][:, cols] > 0.5)
            s = jnp.where(allowed, s, NEG)
            m = m_ref[...]
            m_new = jnp.maximum(m, jnp.max(s, axis=0, keepdims=True))
            alpha = jnp.exp(m - m_new)
            p = jnp.exp(s - m_new)
            l_ref[...] = alpha * l_ref[...] + jnp.sum(p, axis=0, keepdims=True)
            m_ref[...] = m_new
            probs.append((alpha, p.astype(bf16)))
        for (h, c, (_, _, acc_ref)), (alpha, p16) in zip(subs, probs):
            acc_ref[...] = alpha * acc_ref[...] + _dot(vtj[h * hd:(h + 1) * hd, :], p16)

    scores(0, sa_ref)

    def pair(j):
        scores(j + 1, sb_ref)
        softmax_pv(j, sa_ref, False)
        scores(j + 2, sa_ref)
        softmax_pv(j + 1, sb_ref, False)

    def body4(p, carry):
        pair(4 * p)
        pair(4 * p + 2)
        return carry

    def body2(p, carry):
        pair(4 * (i // 4) + 2 * p)
        return carry

    lax.fori_loop(0, i // 4, body4, 0)
    lax.fori_loop(0, (i % 4) // 2, body2, 0)

    @pl.when(i % 2 == 0)
    def _():
        softmax_pv(i, sa_ref, True)

    @pl.when(i % 2 == 1)
    def _():
        scores(i, sb_ref)
        softmax_pv(i - 1, sa_ref, False)
        softmax_pv(i, sb_ref, True)

    o_t = jnp.concatenate(
        [jnp.concatenate([subs[h * halves + c][2][2][...] / subs[h * halves + c][2][1][...] for c in range(halves)], axis=1)
         for h in range(2)], axis=0)
    o_ref[...] = o_t.T.astype(o_ref.dtype)


def _moba(qa, ka, vat, kmean):
    bsz, seq, width = qa.shape
    nblk = seq // MOBA_BLOCK
    pairs = width // LANES
    t = MOBA_BLOCK
    return pl.pallas_call(
        _moba_kernel,
        grid=(bsz, pairs, nblk),
        in_specs=[
            pl.BlockSpec((None, t, LANES), lambda b, p, i: (b, i, p)),
            pl.BlockSpec((None, seq, LANES), lambda b, p, i: (b, 0, p)),
            pl.BlockSpec((None, nblk, LANES, t), lambda b, p, i: (b, 0, p, 0)),
            pl.BlockSpec((None, nblk, LANES), lambda b, p, i: (b, 0, p)),
        ],
        out_specs=pl.BlockSpec((None, t, LANES), lambda b, p, i: (b, i, p)),
        out_shape=jax.ShapeDtypeStruct((bsz, seq, width), bf16),
        scratch_shapes=[pltpu.VMEM((2, nblk, t), f32), pltpu.VMEM((2, t, LANES), bf16),
                        pltpu.VMEM((2, t, t), f32), pltpu.VMEM((2, t, t), f32)] + 2 * (t // LANES) * [
            pltpu.VMEM((1, LANES), f32), pltpu.VMEM((1, LANES), f32), pltpu.VMEM((HEAD_DIM, LANES), f32)],
        compiler_params=_params(("parallel", "parallel", "arbitrary")),
        name="moba_attention",
    )(qa, ka, vat, kmean)


DIL_TILE = 2048


def _dilated_kernel(q_ref, kp_ref, ko_ref, vp_ref, vo_ref, o_ref, kbuf, vbuf, og_ref, lg_ref):
    t = pl.program_id(1)
    tile = q_ref.shape[0]
    w = BAND_BLOCK
    kbuf[0:tile] = kp_ref[...]
    kbuf[tile:2 * tile] = ko_ref[...]
    vbuf[0:tile] = vp_ref[...]
    vbuf[tile:2 * tile] = vo_ref[...]
    qk_h0, v_h0 = _head_masks()
    qi = lax.broadcasted_iota(i32, (w, 2 * w), 0)
    kc = lax.broadcasted_iota(i32, (w, 2 * w), 1)

    def unit_group(g, dil, units):
        span = w * dil
        work = []
        for qs, first in units:
            if dil == 1:
                rows_q, rows_k = pl.ds(qs, w), pl.ds(tile - span + qs, 2 * w)
            else:
                rows_q, rows_k = pl.ds(qs, w, stride=dil), pl.ds(tile - span + qs, 2 * w, stride=dil)
            q = q_ref[rows_q, :]
            kk = kbuf[rows_k, :].astype(bf16)
            allowed = (kc >= jnp.maximum(qi, jnp.where(first, w, 0))) & (kc <= qi + w)
            scores = [_dot_nt(jnp.where(qk_h0 if h == 0 else jnp.logical_not(qk_h0), q, 0.0).astype(bf16), kk)
                      for h in range(2)]
            work.append((rows_q, rows_k, allowed, scores))
        soft = []
        for rows_q, rows_k, allowed, scores in work:
            per_head = []
            for s in scores:
                s = jnp.where(allowed, s, NEG)
                m = jnp.max(s, axis=-1, keepdims=True)
                e = jnp.exp(s - m)
                l = jnp.sum(e, axis=-1, keepdims=True)
                per_head.append((e.astype(bf16), l, m + jnp.log(l)))
            soft.append(per_head)
        for (rows_q, rows_k, _, _), per_head in zip(work, soft):
            vv = vbuf[rows_k, :].astype(bf16)
            outs = [_dot(e16, vv) / l for e16, l, _ in per_head]
            og_ref[g, rows_q, :] = jnp.where(v_h0, outs[0], outs[1])
            lg_ref[g, rows_q, :] = jnp.where(v_h0, per_head[0][2], per_head[1][2])

    first_tile = t == 0
    group = 4
    blocks = tile // w
    for g, (window, dil) in enumerate(DILATED_PAIRS):
        assert window // dil == w and tile % (w * dil) == 0 and blocks % group == 0

        def body(it, carry, g=g, dil=dil):
            units = []
            for uu in range(group):
                if dil == 1:
                    units.append((pl.multiple_of((group * it + uu) * w, w), first_tile & (group * it + uu == 0)))
                elif dil == group:
                    units.append((pl.multiple_of(it * w * dil, w * dil) + uu, first_tile & (it == 0)))
                else:
                    assert tile == w * dil
                    units.append((group * it + uu, first_tile))
            unit_group(g, dil, units)
            return carry

        lax.fori_loop(0, blocks // group, body, 0)

    l0, l1, l2 = lg_ref[0], lg_ref[1], lg_ref[2]
    mx = jnp.maximum(jnp.maximum(l0, l1), l2)
    e0, e1, e2 = jnp.exp(l0 - mx), jnp.exp(l1 - mx), jnp.exp(l2 - mx)
    den = e0 + e1 + e2
    out = (e0 / den) * og_ref[0] + (e1 / den) * og_ref[1] + (e2 / den) * og_ref[2]
    o_ref[...] = out.astype(o_ref.dtype)


def _dilated(qb, kb, vb):
    bsz, seq, width = qb.shape
    tile = DIL_TILE
    assert seq % tile == 0 and tile == BAND_BLOCK * max(d for _, d in DILATED_PAIRS)
    pairs = width // LANES
    own = lambda b, t, p: (b, t, p)
    prev = lambda b, t, p: (b, jnp.maximum(t - 1, 0), p)
    spec = lambda m: pl.BlockSpec((None, tile, LANES), m)
    return pl.pallas_call(
        _dilated_kernel,
        grid=(bsz, seq // tile, pairs),
        in_specs=[spec(own), spec(prev), spec(own), spec(prev), spec(own)],
        out_specs=spec(own),
        out_shape=jax.ShapeDtypeStruct((bsz, seq, width), bf16),
        scratch_shapes=[
            pltpu.VMEM((2 * tile, LANES), f32),
            pltpu.VMEM((2 * tile, LANES), f32),
            pltpu.VMEM((len(DILATED_PAIRS), tile, LANES), f32),
            pltpu.VMEM((len(DILATED_PAIRS), tile, LANES), f32),
        ],
        compiler_params=_params(("parallel", "parallel", "parallel")),
        name="dilated_attention",
    )(qb, kb, kb, vb, vb)


def _out_proj_ln_kernel(*refs, n_parts, alpha):
    o_refs = refs[:n_parts]
    w_refs = refs[n_parts:2 * n_parts]
    x_ref, g_ref, b_ref, y_ref = refs[2 * n_parts:]
    mix = _dot(o_refs[0][...], w_refs[0][...])
    for o_r, w_r in zip(o_refs[1:], w_refs[1:]):
        mix = mix + _dot(o_r[...], w_r[...])
    y_ref[...] = _layer_norm(alpha * x_ref[...] + mix, g_ref[...], b_ref[...])


def _out_proj_ln(parts, weights, x2d, g, b, alpha):
    n, d = x2d.shape
    tm = ROW_TILE
    row = lambda i: (i, 0)
    const = lambda i: (0, 0)
    in_specs = ([pl.BlockSpec((tm, p.shape[1]), row) for p in parts]
                + [pl.BlockSpec(w.shape, const) for w in weights]
                + [pl.BlockSpec((tm, d), row), pl.BlockSpec((1, d), const), pl.BlockSpec((1, d), const)])
    return pl.pallas_call(
        functools.partial(_out_proj_ln_kernel, n_parts=len(parts), alpha=alpha),
        grid=(n // tm,),
        in_specs=in_specs,
        out_specs=pl.BlockSpec((tm, d), row),
        out_shape=jax.ShapeDtypeStruct((n, d), f32),
        compiler_params=_params(("parallel",)),
        name="out_proj_ln",
    )(*parts, *weights, x2d, g.reshape(1, d), b.reshape(1, d))


def _router_kernel(x_ref, wh_ref, wl_ref, bias_ref, idx_ref, gate_ref, cnt_ref, run_ref):
    tm = x_ref.shape[0]
    ne = wh_ref.shape[0]
    eg = EXPERTS_PER_GROUP

    @pl.when(pl.program_id(0) == 0)
    def _():
        run_ref[...] = jnp.zeros(run_ref.shape, f32)

    x_s = _split(x_ref[...])
    wh, wl = wh_ref[...], wl_ref[...]
    logits = _dot_nt(wh, x_s[0]) + (_dot_nt(wh, x_s[1]) + _dot_nt(wl, x_s[0]))
    scores = _sigmoid(logits)
    sel = (scores + bias_ref[...]).reshape(N_GROUPS, eg, tm)
    loc = lax.broadcasted_iota(i32, (N_GROUPS, eg, tm), 1)
    m1 = jnp.max(sel, axis=1, keepdims=True)
    i1 = jnp.min(jnp.where(sel == m1, loc, eg), axis=1, keepdims=True)
    rest = jnp.where(loc == i1, -jnp.inf, sel)
    m2 = jnp.max(rest, axis=1, keepdims=True)
    i2 = jnp.min(jnp.where(rest == m2, loc, eg), axis=1, keepdims=True)
    gs = m1 + m2
    best, grp, l1, l2 = gs[0], jnp.zeros((1, tm), i32), i1[0], i2[0]
    for g in range(1, N_GROUPS):
        better = gs[g] > best
        best = jnp.where(better, gs[g], best)
        grp = jnp.where(better, g, grp)
        l1 = jnp.where(better, i1[g], l1)
        l2 = jnp.where(better, i2[g], l2)
    e1 = grp * eg + l1
    e2 = grp * eg + l2
    row = lax.broadcasted_iota(i32, (ne, tm), 0)
    oh1 = row == e1
    oh2 = row == e2
    g1 = jnp.sum(jnp.where(oh1, scores, 0.0), axis=0, keepdims=True)
    g2 = jnp.sum(jnp.where(oh2, scores, 0.0), axis=0, keepdims=True)
    gsum = g1 + g2
    both = jnp.where(oh1 | oh2, 1.0, 0.0)
    r_i = lax.broadcasted_iota(i32, (tm, tm), 0)
    c_i = lax.broadcasted_iota(i32, (tm, tm), 1)
    tri = jnp.where(r_i < c_i, 1.0, 0.0).astype(bf16)
    before = _dot(both.astype(bf16), tri) + run_ref[...]
    r1 = jnp.sum(jnp.where(oh1, before, 0.0), axis=0, keepdims=True)
    r2 = jnp.sum(jnp.where(oh2, before, 0.0), axis=0, keepdims=True)
    run_ref[...] = run_ref[...] + jnp.sum(both, axis=1, keepdims=True)
    pad = SUBLANES - 2 * TOP_K
    idx_ref[...] = jnp.concatenate([e1.astype(f32), e2.astype(f32), r1, r2] + pad * [r2], axis=0).astype(i32)
    gate_ref[...] = jnp.concatenate([g1 / gsum, g2 / gsum] + (SUBLANES - TOP_K) * [g2], axis=0)
    cnt_ref[...] = run_ref[...].astype(i32)


def _router(x2d, router_w, router_bias):
    n, d = x2d.shape
    tm = ROW_TILE
    ne = router_w.shape[1]
    wt = router_w.T
    wt_hi = wt.astype(bf16)
    wt_lo = (wt - wt_hi.astype(f32)).astype(bf16)
    return pl.pallas_call(
        _router_kernel,
        grid=(n // tm,),
        in_specs=[
            pl.BlockSpec((tm, d), lambda i: (i, 0)),
            pl.BlockSpec((ne, d), lambda i: (0, 0)),
            pl.BlockSpec((ne, d), lambda i: (0, 0)),
            pl.BlockSpec((ne, 1), lambda i: (0, 0)),
        ],
        out_specs=(
            pl.BlockSpec((SUBLANES, tm), lambda i: (0, i)),
            pl.BlockSpec((SUBLANES, tm), lambda i: (0, i)),
            pl.BlockSpec((ne, 1), lambda i: (0, 0)),
        ),
        out_shape=(
            jax.ShapeDtypeStruct((SUBLANES, n), i32),
            jax.ShapeDtypeStruct((SUBLANES, n), f32),
            jax.ShapeDtypeStruct((ne, 1), i32),
        ),
        scratch_shapes=[pltpu.VMEM((ne, 1), f32)],
        compiler_params=_params(("arbitrary",)),
        name="moe_router",
    )(x2d, wt_hi, wt_lo, router_bias.reshape(ne, 1))


def _row_dma_loops(tm, copy):
    def start(g, c):
        base = pl.multiple_of(g * SUBLANES, SUBLANES)
        for r in range(SUBLANES):
            for k in range(TOP_K):
                copy(base + r, k).start()
        return c

    def wait(g, c):
        base = pl.multiple_of(g * SUBLANES, SUBLANES)
        for r in range(SUBLANES):
            for k in range(TOP_K):
                copy(base + r, k).wait()
        return c

    lax.fori_loop(0, tm // SUBLANES, start, 0)
    lax.fori_loop(0, tm // SUBLANES, wait, 0)


def _to_tile_major(dst_ref, value):
    rows = value.shape[0]
    for c in range(value.shape[1] // LANES):
        dst_ref[pl.ds(c, rows, stride=SUBLANES), :] = value[:, c * LANES:(c + 1) * LANES]


def _from_tile_major(src_ref, rows, chunks):
    return jnp.concatenate([src_ref[pl.ds(c, rows, stride=SUBLANES), :] for c in range(chunks)], axis=1)


def _dispatch_kernel(d0_ref, d1_ref, x_ref, xs_in, xs_hbm, x3, sem):
    del xs_in
    tm = x_ref.shape[0]
    dests = (d0_ref, d1_ref)
    _to_tile_major(x3, x_ref[...])

    def copy(t, k):
        return pltpu.make_async_copy(x3.at[pl.ds(t * SUBLANES, SUBLANES)],
                                     xs_hbm.at[pl.ds(pl.multiple_of(dests[k][t], SUBLANES), SUBLANES)], sem)

    _row_dma_loops(tm, copy)


def _dispatch(x2d, dest8, cap):
    n, d = x2d.shape
    assert d == SUBLANES * LANES
    tm = ROW_TILE
    xs0 = jnp.zeros((cap * SUBLANES, LANES), x2d.dtype)
    return pl.pallas_call(
        _dispatch_kernel,
        grid=(n // tm,),
        in_specs=[
            pl.BlockSpec((tm,), lambda i: (i,), memory_space=pltpu.SMEM),
            pl.BlockSpec((tm,), lambda i: (i,), memory_space=pltpu.SMEM),
            pl.BlockSpec((tm, d), lambda i: (i, 0)),
            pl.BlockSpec(memory_space=pl.ANY),
        ],
        out_specs=pl.BlockSpec(memory_space=pl.ANY),
        out_shape=jax.ShapeDtypeStruct((cap * SUBLANES, LANES), x2d.dtype),
        scratch_shapes=[pltpu.VMEM((tm * SUBLANES, LANES), x2d.dtype), pltpu.SemaphoreType.DMA(())],
        input_output_aliases={3: 0},
        compiler_params=_params(("arbitrary",), has_side_effects=True, disable_bounds_checks=True),
        name="moe_dispatch",
    )(dest8[0], dest8[1], x2d, xs0)


def _expert_kernel(be_ref, nused_ref, xs_ref, w1_ref, w3_ref, w2_ref, ys_ref, w1b, w3b, w2b):
    b = pl.program_id(0)
    changed = jnp.logical_or(b == 0, be_ref[b] != be_ref[jnp.maximum(b - 1, 0)])

    @pl.when(changed)
    def _():
        w1b[...] = w1_ref[...].astype(bf16)
        w3b[...] = w3_ref[...].astype(bf16)
        w2b[...] = w2_ref[...].astype(bf16)

    @pl.when(b < nused_ref[0])
    def _():
        xb = _from_tile_major(xs_ref, MOE_BLOCK, w1b.shape[0] // LANES).astype(bf16)
        h1 = _dot(xb, w1b[...])
        h3 = _dot(xb, w3b[...])
        hb = (h1 * _sigmoid(h1)) * h3
        _to_tile_major(ys_ref, _dot(hb.astype(bf16), w2b[...]))

    @pl.when(b >= nused_ref[0])
    def _():
        ys_ref[...] = jnp.zeros(ys_ref.shape, f32)


def _expert_ffn(xs, block_expert, n_used, w1, w3, w2, layer):
    d, dff = w1.shape[-2:]
    cap = xs.shape[0] // SUBLANES
    nblocks = cap // MOE_BLOCK
    tile_rows = MOE_BLOCK * SUBLANES
    grid_spec = pltpu.PrefetchScalarGridSpec(
        num_scalar_prefetch=2,
        grid=(nblocks,),
        in_specs=[
            pl.BlockSpec((tile_rows, LANES), lambda b, be, nu: (b, 0)),
            pl.BlockSpec((None, None, d, dff), lambda b, be, nu: (layer, be[b], 0, 0)),
            pl.BlockSpec((None, None, d, dff), lambda b, be, nu: (layer, be[b], 0, 0)),
            pl.BlockSpec((None, None, dff, d), lambda b, be, nu: (layer, be[b], 0, 0)),
        ],
        out_specs=pl.BlockSpec((tile_rows, LANES), lambda b, be, nu: (b, 0)),
        scratch_shapes=[pltpu.VMEM((d, dff), bf16), pltpu.VMEM((d, dff), bf16), pltpu.VMEM((dff, d), bf16)],
    )
    return pl.pallas_call(
        _expert_kernel,
        grid_spec=grid_spec,
        out_shape=jax.ShapeDtypeStruct((cap * SUBLANES, LANES), f32),
        compiler_params=_params(("arbitrary",)),
        name="moe_expert_ffn",
    )(block_expert, n_used, xs, w1, w3, w2)


def _combine_kernel(d0_ref, d1_ref, x_ref, gate_ref, g_ref, b_ref, ys_hbm, y_ref, buf, sem, *, alpha):
    tm = x_ref.shape[0]
    dests = (d0_ref, d1_ref)

    def copy(t, k):
        return pltpu.make_async_copy(ys_hbm.at[pl.ds(pl.multiple_of(dests[k][t], SUBLANES), SUBLANES)],
                                     buf.at[k, pl.ds(t * SUBLANES, SUBLANES)], sem)

    _row_dma_loops(tm, copy)
    gates = gate_ref[...].T
    chunks = x_ref.shape[1] // LANES
    ffn = (_from_tile_major(buf.at[0], tm, chunks) * gates[:, 0:1]
           + _from_tile_major(buf.at[1], tm, chunks) * gates[:, 1:2])
    y_ref[...] = _layer_norm(alpha * x_ref[...] + ffn, g_ref[...], b_ref[...])


def _combine_ln(x2d, ys, dest, gates, g, b, alpha):
    n, d = x2d.shape
    tm = ROW_TILE
    return pl.pallas_call(
        functools.partial(_combine_kernel, alpha=alpha),
        grid=(n // tm,),
        in_specs=[
            pl.BlockSpec((tm,), lambda i: (i,), memory_space=pltpu.SMEM),
            pl.BlockSpec((tm,), lambda i: (i,), memory_space=pltpu.SMEM),
            pl.BlockSpec((tm, d), lambda i: (i, 0)),
            pl.BlockSpec((SUBLANES, tm), lambda i: (0, i)),
            pl.BlockSpec((1, d), lambda i: (0, 0)),
            pl.BlockSpec((1, d), lambda i: (0, 0)),
            pl.BlockSpec(memory_space=pl.ANY),
        ],
        out_specs=pl.BlockSpec((tm, d), lambda i: (i, 0)),
        out_shape=jax.ShapeDtypeStruct((n, d), f32),
        scratch_shapes=[pltpu.VMEM((TOP_K, tm * SUBLANES, LANES), f32), pltpu.SemaphoreType.DMA(())],
        compiler_params=_params(("arbitrary",), disable_bounds_checks=True),
        name="moe_combine_ln",
    )(dest[0], dest[1], x2d, gates, g.reshape(1, d), b.reshape(1, d), ys)


def _moe_ln(x2d, router_w, router_bias, w1, w3, w2, layer, g, b, alpha):
    n, d = x2d.shape
    idx, gates, counts = _router(x2d, router_w, router_bias)
    counts = counts.reshape(-1)
    padded = (counts + MOE_BLOCK - 1) // MOE_BLOCK * MOE_BLOCK
    pad_end = jnp.cumsum(padded)
    pad_start = pad_end - padded
    nblocks = n * TOP_K // MOE_BLOCK + N_EXPERTS
    cap = nblocks * MOE_BLOCK
    block_start = jnp.arange(nblocks, dtype=i32) * MOE_BLOCK
    block_expert = jnp.minimum(jnp.sum(pad_end[None, :] <= block_start[:, None], axis=1), N_EXPERTS - 1).astype(i32)
    n_used = (pad_end[-1:] // MOE_BLOCK).astype(i32)
    experts = jnp.arange(N_EXPERTS, dtype=i32)[:, None]
    dest = [((jnp.sum(jnp.where(idx[k][None, :] == experts, pad_start[:, None], 0), axis=0) + idx[TOP_K + k])
             * SUBLANES).astype(i32) for k in range(TOP_K)]
    xs = _dispatch(x2d, dest, cap)
    ys = _expert_ffn(xs, block_expert, n_used, w1, w3, w2, layer)
    return _combine_ln(x2d, ys, dest, gates, g, b, alpha)


def _gdn_proj_kernel(x_ref, wc_ref, wz_ref, wba_ref, cw_ref, alog_ref, dt_ref,
                     qkv_ref, z_ref, beta_ref, gc_ref, buf, *, tiles_per_seq):
    i = pl.program_id(0)
    tm = x_ref.shape[0]
    halo = 8
    xb = x_ref[...].astype(bf16)
    qk_cols = 2 * GDN_QK_HEADS * GDN_HEAD_DIM
    conv_ch = wc_ref.shape[1]

    @pl.when(i % tiles_per_seq == 0)
    def _():
        buf[0:halo, :] = jnp.zeros((halo, conv_ch), f32)

    buf[halo:halo + tm, :] = _dot(xb, wc_ref[...])
    q_scale = GDN_HEAD_DIM ** -0.5
    for c in range(conv_ch // LANES):
        cols = slice(c * LANES, (c + 1) * LANES)
        y = jnp.zeros((tm, LANES), f32)
        for j in range(GDN_CONV):
            shift = GDN_CONV - 1 - j
            y = y + buf[halo - shift:halo - shift + tm, cols] * cw_ref[j:j + 1, cols]
        y = y * _sigmoid(y)
        if c * LANES < qk_cols:
            y = y * lax.rsqrt(jnp.sum(y * y, axis=-1, keepdims=True) + NORM_EPS)
            if c * LANES < qk_cols // 2:
                y = y * q_scale
        qkv_ref[:, cols] = y
    buf[0:halo, :] = buf[tm:tm + halo, :]

    z_ref[...] = _dot(xb, wz_ref[...])
    ba = _dot_nt(wba_ref[...], xb)
    hv = GDN_V_HEADS
    beta_ref[...] = _sigmoid(ba[:hv])
    a = ba[hv:] + dt_ref[...]
    softplus = jnp.maximum(a, 0.0) + jnp.log(1.0 + jnp.exp(-jnp.abs(a)))
    g = -jnp.exp(alog_ref[...]) * softplus
    r_i = lax.broadcasted_iota(i32, (tm, tm), 0)
    c_i = lax.broadcasted_iota(i32, (tm, tm), 1)
    same_chunk = (r_i // GDN_CHUNK) == (c_i // GDN_CHUNK)
    cum = jnp.where(same_chunk & (r_i <= c_i), 1.0, 0.0)
    gc_ref[...] = _dot_hi(g, cum)


def _gdn_proj(x2d, w_conv, w_z, w_ba_t, conv_w, a_log, dt_bias, seq):
    n, d = x2d.shape
    tm = ROW_TILE // 2
    conv_ch = w_conv.shape[1]
    vdim = w_z.shape[1]
    hv = GDN_V_HEADS
    row = lambda i: (i, 0)
    const = lambda i: (0, 0)
    return pl.pallas_call(
        functools.partial(_gdn_proj_kernel, tiles_per_seq=seq // tm),
        grid=(n // tm,),
        in_specs=[
            pl.BlockSpec((tm, d), row),
            pl.BlockSpec(w_conv.shape, const, pipeline_mode=pl.Buffered(1)),
            pl.BlockSpec(w_z.shape, const, pipeline_mode=pl.Buffered(1)),
            pl.BlockSpec(w_ba_t.shape, const),
            pl.BlockSpec(conv_w.shape, const),
            pl.BlockSpec((hv, 1), const),
            pl.BlockSpec((hv, 1), const),
        ],
        out_specs=(
            pl.BlockSpec((tm, conv_ch), row),
            pl.BlockSpec((tm, vdim), row),
            pl.BlockSpec((hv, tm), lambda i: (0, i)),
            pl.BlockSpec((hv, tm), lambda i: (0, i)),
        ),
        out_shape=(
            jax.ShapeDtypeStruct((n, conv_ch), f32),
            jax.ShapeDtypeStruct((n, vdim), f32),
            jax.ShapeDtypeStruct((hv, n), f32),
            jax.ShapeDtypeStruct((hv, n), f32),
        ),
        scratch_shapes=[pltpu.VMEM((tm + 16, conv_ch), f32)],
        compiler_params=_params(("arbitrary",)),
        name="gdn_proj",
    )(x2d, w_conv, w_z, w_ba_t, conv_w, a_log.reshape(hv, 1), dt_bias.reshape(hv, 1))


def _gdn_delta_kernel(q_ref, k_ref, v_ref, z_ref, betar_ref, gcr_ref, ng_ref, o_ref, state):
    ts = q_ref.shape[0]
    c = GDN_CHUNK

    @pl.when(pl.program_id(2) == 0)
    def _():
        state[...] = jnp.zeros(state.shape, f32)

    hb = state.shape[0]
    hd = GDN_HEAD_DIM
    sub = GDN_SOLVE
    nch = sub // c
    r_i = lax.broadcasted_iota(i32, (sub, sub), 0)
    c_i = lax.broadcasted_iota(i32, (sub, sub), 1)
    same = (r_i // c) == (c_i // c)
    lower = same & (r_i >= c_i)
    strict = same & (r_i > c_i)
    inst = [(st, h) for st in range(ts // sub) for h in range(hb)]
    hs = range(len(inst))
    rows_of = [slice(st * sub, (st + 1) * sub) for st, _ in inst]
    q = [q_ref[rows_of[i], :] for i in hs]
    k = [k_ref[rows_of[i], :] for i in hs]
    k16 = [k[i].astype(bf16) for i in hs]
    qk = [_dot_nt(q[i].astype(bf16), k16[i]) for i in hs]

    rows8 = jnp.concatenate([betar_ref[h] for h in range(hb)] + [gcr_ref[h] for h in range(hb)]
                            + (SUBLANES - 2 * hb) * [gcr_ref[0]], axis=0)
    cols8 = rows8.T
    beta = [cols8[rows_of[i], h:h + 1] for i, (_, h) in enumerate(inst)]
    gcc = [cols8[rows_of[i], hb + h:hb + h + 1] for i, (_, h) in enumerate(inst)]
    gcr = [gcr_ref[h][:, rows_of[i]] for i, (_, h) in enumerate(inst)]
    decay = [jnp.where(lower, jnp.exp(jnp.where(lower, gcc[h] - gcr[h], 0.0)), 0.0) for h in hs]
    egc = [jnp.exp(gcc[h]) for h in hs]
    kb = [k[h] * beta[h] for h in hs]
    kk = [_dot_nt(kb[h].astype(bf16), k16[h]) for h in hs]
    l16 = [jnp.where(strict, kk[h] * decay[h], 0.0).astype(bf16) for h in hs]
    rhs = [jnp.concatenate([v_ref[rows_of[i], h * hd:(h + 1) * hd] * beta[i], kb[i] * egc[i]], axis=-1)
           for i, (_, h) in enumerate(inst)]
    pows = [l16]
    span = 2
    while span < c:
        pows.append([_dot(pows[-1][h], pows[-1][h]).astype(bf16) for h in hs])
        span *= 2
    sol = [rhs[h] - _dot(pows[0][h], rhs[h].astype(bf16)) for h in hs]
    for p in pows[1:]:
        sol = [sol[h] + _dot(p[h], sol[h].astype(bf16)) for h in hs]
    sol16 = [sol[h].astype(bf16) for h in hs]
    auw = [_dot((qk[h] * decay[h]).astype(bf16), sol16[h]) for h in hs]
    parts = []
    for i in hs:
        o0 = auw[i][:, :hd]
        qe16 = (q[i] * egc[i] - auw[i][:, hd:]).astype(bf16)
        ab, cd = [], []
        for n in range(nch):
            rows = slice(n * c, (n + 1) * c)
            g_last = gcr[i][:, (n + 1) * c - 1:(n + 1) * c]
            kd16 = (k[i][rows] * jnp.exp(g_last - gcc[i][rows])).astype(bf16)
            ab.append(_dot_tn(kd16, sol16[i][rows]))
            cd.append(jnp.exp(g_last))
        parts.append((o0, qe16, ab, cd))

    outs = [[] for _ in range(hb)]
    s_cur = [state[h] for h in range(hb)]
    for st in range(ts // sub):
        for n in range(nch):
            rows = slice(n * c, (n + 1) * c)
            for h in range(hb):
                o0, qe16, ab, cd = parts[st * hb + h]
                s16 = s_cur[h].astype(bf16)
                outs[h].append(_dot(qe16[rows], s16) + o0[rows])
                s_cur[h] = s_cur[h] * cd[n] - _dot(ab[n][:, hd:].astype(bf16), s16) + ab[n][:, :hd]
    for h in range(hb):
        state[h] = s_cur[h]
        o = jnp.concatenate(outs[h], axis=0)
        z = z_ref[:, h * hd:(h + 1) * hd]
        o = o * lax.rsqrt(jnp.mean(o * o, axis=-1, keepdims=True) + NORM_EPS) * ng_ref[...] * (z * _sigmoid(z))
        o_ref[:, h * hd:(h + 1) * hd] = o.astype(o_ref.dtype)


def _gdn_delta(qkv, z, beta_row, gc_row, norm_g, ts):
    bsz, seq, _ = qkv.shape
    hd = GDN_HEAD_DIM
    rep = GDN_V_HEADS // GDN_QK_HEADS
    tiles = seq // ts
    row_spec = pl.BlockSpec((rep, None, 1, ts), lambda b, g, s: (g, b * tiles + s, 0, 0))
    v_off = 2 * GDN_QK_HEADS // rep
    return pl.pallas_call(
        _gdn_delta_kernel,
        grid=(bsz, GDN_QK_HEADS, seq // ts),
        in_specs=[
            pl.BlockSpec((None, ts, hd), lambda b, g, s: (b, s, g)),
            pl.BlockSpec((None, ts, hd), lambda b, g, s: (b, s, GDN_QK_HEADS + g)),
            pl.BlockSpec((None, ts, rep * hd), lambda b, g, s: (b, s, v_off + g)),
            pl.BlockSpec((None, ts, rep * hd), lambda b, g, s: (b, s, g)),
            row_spec,
            row_spec,
            pl.BlockSpec((1, hd), lambda b, g, s: (0, 0)),
        ],
        out_specs=pl.BlockSpec((None, ts, rep * hd), lambda b, g, s: (b, s, g)),
        out_shape=jax.ShapeDtypeStruct((bsz, seq, GDN_V_HEADS * hd), bf16),
        scratch_shapes=[pltpu.VMEM((rep, hd, hd), f32)],
        compiler_params=_params(("parallel", "parallel", "arbitrary")),
        name="gdn_delta",
    )(qkv, qkv, qkv, z, beta_row, gc_row, norm_g.reshape(1, hd))


def _rope_slab_tables(seq):
    half = HEAD_DIM // 2
    inv = ROPE_THETA ** (-jnp.arange(0, HEAD_DIM, 2, dtype=f32) / HEAD_DIM)
    ang = jnp.arange(seq, dtype=f32)[:, None] * inv[None, :]
    cos, sin = jnp.cos(ang), jnp.sin(ang)
    cos_t = jnp.concatenate([cos] * (LANES // half), axis=1)
    sin_t = jnp.concatenate([-sin, -sin, sin, sin], axis=1)
    return cos_t, sin_t


def _attention_layer(x2d, bsz, seq, w_in, w_out, cos_t, sin_t, g, b, alpha):
    n, d = x2d.shape
    w_perm = w_in[:, _ab_col_perm()].astype(bf16)
    qa, ka, vat, qb, kb, vb, kmean = _ab_proj(x2d, w_perm, cos_t, sin_t, seq)
    sh = lambda t: t.reshape(bsz, seq, t.shape[-1])
    nblk = seq // MOBA_BLOCK
    o_a = _moba(sh(qa), sh(ka), vat.reshape(bsz, nblk, -1, MOBA_BLOCK), kmean.reshape(bsz, nblk, -1))
    o_b = _dilated(sh(qb), sh(kb), sh(vb))
    na = N_HEADS_A * HEAD_DIM
    w_o = w_out.astype(bf16)
    return _out_proj_ln([o_a.reshape(n, -1), o_b.reshape(n, -1)], [w_o[:na], w_o[na:]], x2d, g, b, alpha)


def _gdn_layer(x2d, bsz, seq, w_in, conv_w, a_log, dt_bias, norm_g, w_out, g, b, alpha):
    n, d = x2d.shape
    conv_ch = conv_w.shape[1]
    vdim = GDN_V_HEADS * GDN_HEAD_DIM
    w16 = w_in.astype(bf16)
    qkv, z, beta, gc = _gdn_proj(x2d, w16[:, :conv_ch], w16[:, conv_ch:conv_ch + vdim], w16[:, conv_ch + vdim:].T,
                                 conv_w, a_log, dt_bias, seq)
    rows = lambda t: t.reshape(GDN_V_HEADS, n // GDN_TILE, 1, GDN_TILE)
    o = _gdn_delta(qkv.reshape(bsz, seq, -1), z.reshape(bsz, seq, -1), rows(beta), rows(gc), norm_g, GDN_TILE)
    return _out_proj_ln([o.reshape(n, -1)], [w_out.astype(bf16)], x2d, g, b, alpha)


def kernel(x, ab_w_in, ab_w_out, gdn_w_in, gdn_conv_w, gdn_a_log, gdn_dt_bias, gdn_norm_g, gdn_w_out, mix_ln_g, mix_ln_b, router_w, router_bias, moe_w1, moe_w3, moe_w2, ffn_ln_g, ffn_ln_b):
    bsz, seq, d = x.shape
    depth = mix_ln_g.shape[0]
    alpha = (2.0 * depth) ** 0.25
    cos_t, sin_t = _rope_slab_tables(seq)
    h = x.reshape(bsz * seq, d)
    for layer in range(depth):
        j = layer // 2
        if layer % 2 == 0:
            h = _attention_layer(h, bsz, seq, ab_w_in[j], ab_w_out[j], cos_t, sin_t,
                                 mix_ln_g[layer], mix_ln_b[layer], alpha)
        else:
            h = _gdn_layer(h, bsz, seq, gdn_w_in[j], gdn_conv_w[j], gdn_a_log[j], gdn_dt_bias[j], gdn_norm_g[j],
                           gdn_w_out[j], mix_ln_g[layer], mix_ln_b[layer], alpha)
        h = _moe_ln(h, router_w, router_bias, moe_w1, moe_w3, moe_w2, layer, ffn_ln_g[layer], ffn_ln_b[layer], alpha)
    return h.reshape(bsz, seq, d)
```

```python
import functools

import numpy as np
import jax
import jax.numpy as jnp
from jax import lax
from jax.experimental import pallas as pl
from jax.experimental.pallas import tpu as pltpu

f32 = jnp.float32
bf16 = jnp.bfloat16
i32 = jnp.int32

HEAD_DIM = 64
N_HEADS_A = 8
N_HEADS_B = 8
AB_HEADS = N_HEADS_A + N_HEADS_B
MOBA_BLOCK = 256
MOBA_TOPK = 3
DILATED_PAIRS = ((128, 1), (512, 4), (2048, 16))
BAND_BLOCK = 128
ROPE_THETA = 10000.0

GDN_QK_HEADS = 8
GDN_V_HEADS = 16
GDN_HEAD_DIM = 128
GDN_CONV = 4
GDN_CHUNK = 64

N_EXPERTS = 32
N_GROUPS = 4
EXPERTS_PER_GROUP = N_EXPERTS // N_GROUPS
TOP_K = 2
MOE_BLOCK = 256

LN_EPS = 1e-5
NORM_EPS = 1e-6
NEG = -1e30

LANES = 128
SUBLANES = 8
ROW_TILE = 512
GDN_SOLVE = 256
GDN_TILE = 1024
VMEM_LIMIT = 56 * 1024 * 1024


def _params(semantics, **kw):
    return pltpu.CompilerParams(dimension_semantics=semantics, vmem_limit_bytes=VMEM_LIMIT, **kw)


def _dot(a, b):
    return jnp.dot(a, b, preferred_element_type=f32)


def _dot_nt(a, b):
    return lax.dot_general(a, b, (((1,), (1,)), ((), ())), preferred_element_type=f32)


def _dot_tn(a, b):
    return lax.dot_general(a, b, (((0,), (0,)), ((), ())), preferred_element_type=f32)


def _dot_hi(a, b):
    return jnp.dot(a, b, preferred_element_type=f32, precision=lax.Precision.HIGHEST)


def _split(a):
    hi = a.astype(bf16)
    return hi, (a - hi.astype(f32)).astype(bf16)


def _layer_norm(y, g, b):
    mu = jnp.mean(y, axis=-1, keepdims=True)
    d = y - mu
    var = jnp.mean(d * d, axis=-1, keepdims=True)
    return d * lax.rsqrt(var + LN_EPS) * g + b


def _sigmoid(x):
    return 1.0 / (1.0 + jnp.exp(-x))


def _ab_col_perm():
    perm = np.zeros((3, AB_HEADS * HEAD_DIM), np.int32)
    half = HEAD_DIM // 2
    for t in range(3):
        for col in range(AB_HEADS * HEAD_DIM):
            if t == 2:
                perm[t, col] = t * AB_HEADS * HEAD_DIM + col
                continue
            slab, lane = divmod(col, LANES)
            part, within = divmod(lane, half)
            head = 2 * slab + (part % 2)
            dim = within + half * (part // 2)
            perm[t, col] = t * AB_HEADS * HEAD_DIM + head * HEAD_DIM + dim
    return perm.reshape(-1)


def _ab_proj_kernel(x_ref, w_ref, cos_ref, sin_ref, qa_ref, ka_ref, vat_ref, qb_ref, kb_ref, vb_ref, km_ref):
    tm = x_ref.shape[0]
    width = AB_HEADS * HEAD_DIM
    na = N_HEADS_A * HEAD_DIM
    xb = x_ref[...].astype(bf16)
    c = cos_ref[...]
    s = sin_ref[...]
    scale = HEAD_DIM ** -0.5

    def rope(slab):
        return slab * c + pltpu.roll(slab, LANES // 2, 1) * s

    acc = _dot(xb, w_ref[:, 0:width])
    for j in range(width // LANES):
        r = rope(acc[:, j * LANES:(j + 1) * LANES]) * scale
        if j * LANES < na:
            qa_ref[:, j * LANES:(j + 1) * LANES] = r.astype(bf16)
        else:
            qb_ref[:, j * LANES - na:(j + 1) * LANES - na] = r
    acc = _dot(xb, w_ref[:, width:2 * width])
    for j in range(width // LANES):
        r = rope(acc[:, j * LANES:(j + 1) * LANES])
        if j * LANES < na:
            ka_ref[:, j * LANES:(j + 1) * LANES] = r.astype(bf16)
            for blk in range(tm // MOBA_BLOCK):
                km_ref[blk, :, j * LANES:(j + 1) * LANES] = jnp.mean(
                    r[blk * MOBA_BLOCK:(blk + 1) * MOBA_BLOCK], axis=0, keepdims=True)
        else:
            kb_ref[:, j * LANES - na:(j + 1) * LANES - na] = r
    acc = _dot(xb, w_ref[:, 2 * width:3 * width])
    for blk in range(tm // MOBA_BLOCK):
        vat_ref[blk] = acc[blk * MOBA_BLOCK:(blk + 1) * MOBA_BLOCK, :na].T.astype(bf16)
    vb_ref[...] = acc[:, na:]


def _ab_proj(x2d, w_perm, cos_t, sin_t, seq):
    n, d = x2d.shape
    tm = ROW_TILE
    na = N_HEADS_A * HEAD_DIM
    nb = N_HEADS_B * HEAD_DIM
    tiles_per_seq = seq // tm
    row = lambda i: (i, 0)
    out_shape = (
        jax.ShapeDtypeStruct((n, na), bf16), jax.ShapeDtypeStruct((n, na), bf16),
        jax.ShapeDtypeStruct((n // MOBA_BLOCK, na, MOBA_BLOCK), bf16),
        jax.ShapeDtypeStruct((n, nb), f32), jax.ShapeDtypeStruct((n, nb), f32), jax.ShapeDtypeStruct((n, nb), f32),
        jax.ShapeDtypeStruct((n // MOBA_BLOCK, 1, na), f32),
    )
    return pl.pallas_call(
        _ab_proj_kernel,
        grid=(n // tm,),
        in_specs=[
            pl.BlockSpec((tm, d), row),
            pl.BlockSpec(w_perm.shape, lambda i: (0, 0)),
            pl.BlockSpec((tm, LANES), lambda i: (i % tiles_per_seq, 0)),
            pl.BlockSpec((tm, LANES), lambda i: (i % tiles_per_seq, 0)),
        ],
        out_specs=(
            pl.BlockSpec((tm, na), row), pl.BlockSpec((tm, na), row),
            pl.BlockSpec((tm // MOBA_BLOCK, na, MOBA_BLOCK), lambda i: (i, 0, 0)),
            pl.BlockSpec((tm, nb), row), pl.BlockSpec((tm, nb), row), pl.BlockSpec((tm, nb), row),
            pl.BlockSpec((tm // MOBA_BLOCK, 1, na), lambda i: (i, 0, 0)),
        ),
        out_shape=out_shape,
        compiler_params=_params(("parallel",)),
        name="ab_proj",
    )(x2d, w_perm, cos_t, sin_t)


def _head_masks():
    lane = lax.broadcasted_iota(i32, (1, LANES), 1)
    qk_h0 = (lane // (HEAD_DIM // 2)) % 2 == 0
    v_h0 = lane < HEAD_DIM
    return qk_h0, v_h0


def _moba_kernel(q_ref, k_ref, vt_ref, km_ref, o_ref, sel_ref, qh_ref, sa_ref, sb_ref, *state):
    i = pl.program_id(2)
    tq = q_ref.shape[0]
    tk = MOBA_BLOCK
    nblk = km_ref.shape[0]
    hd = HEAD_DIM
    qk_h0, _ = _head_masks()
    q = q_ref[...]
    qh = (jnp.where(qk_h0, q, jnp.zeros_like(q)), jnp.where(qk_h0, jnp.zeros_like(q), q))

    km = km_ref[...]
    km_hi = km.astype(bf16)
    km_lo = (km - km_hi.astype(f32)).astype(bf16)
    blk = lax.broadcasted_iota(i32, (nblk, tq), 0)
    for h in range(2):
        gate = _dot_nt(km_hi, qh[h]) + _dot_nt(km_lo, qh[h])
        gate = jnp.where(blk < i, gate, NEG)
        sel = jnp.zeros((nblk, tq), f32)
        for _ in range(MOBA_TOPK):
            mx = jnp.max(gate, axis=0, keepdims=True)
            idx = jnp.min(jnp.where(gate == mx, blk, nblk), axis=0, keepdims=True)
            sel = jnp.where(blk == jnp.where(idx < i, idx, -1), 1.0, sel)
            gate = jnp.where(blk == idx, -jnp.inf, gate)
        sel_ref[h] = sel

    halves = tq // LANES
    subs = [(h, c, state[3 * (h * halves + c):3 * (h * halves + c) + 3]) for h in range(2) for c in range(halves)]
    for _, _, (m_ref, l_ref, acc_ref) in subs:
        m_ref[...] = jnp.full(m_ref.shape, -jnp.inf, f32)
        l_ref[...] = jnp.zeros(l_ref.shape, f32)
        acc_ref[...] = jnp.zeros(acc_ref.shape, f32)
    for h in range(2):
        qh_ref[h] = qh[h]
    k_io = lax.broadcasted_iota(i32, (tk, LANES), 0)
    q_io = lax.broadcasted_iota(i32, (tk, LANES), 1)

    def scores(j, s_ref):
        kj = k_ref[pl.ds(pl.multiple_of(j * tk, tk), tk), :]
        for h in range(2):
            s_ref[h] = _dot_nt(kj, qh_ref[h])

    def softmax_pv(j, s_ref, own):
        vtj = vt_ref[j]
        sel_rows = None if own else [sel_ref[h, pl.ds(j, 1), :] for h in range(2)]
        probs = []
        for h, c, (m_ref, l_ref, _) in subs:
            cols = slice(c * LANES, (c + 1) * LANES)
            s = s_ref[h, :, cols]
            allowed = (k_io <= q_io + c * LANES) if own else (sel_rows[h][:, cols] > 0.5)
            s = jnp.where(allowed, s, NEG)
            m = m_ref[...]
            m_new = jnp.maximum(m, jnp.max(s, axis=0, keepdims=True))
            alpha = jnp.exp(m - m_new)
            p = jnp.exp(s - m_new)
            l_ref[...] = alpha * l_ref[...] + jnp.sum(p, axis=0, keepdims=True)
            m_ref[...] = m_new
            probs.append((alpha, p.astype(bf16)))
        for (h, c, (_, _, acc_ref)), (alpha, p16) in zip(subs, probs):
            acc_ref[...] = alpha * acc_ref[...] + _dot(vtj[h * hd:(h + 1) * hd, :], p16)

    scores(0, sa_ref)

    def pair(j):
        scores(j + 1, sb_ref)
        softmax_pv(j, sa_ref, False)
        scores(j + 2, sa_ref)
        softmax_pv(j + 1, sb_ref, False)

    def body8(p, carry):
        for u in range(0, 8, 2):
            pair(8 * p + u)
        return carry

    def body4(p, carry):
        pair(8 * (i // 8) + 4 * p)
        pair(8 * (i // 8) + 4 * p + 2)
        return carry

    def body2(p, carry):
        pair(4 * (i // 4) + 2 * p)
        return carry

    lax.fori_loop(0, i // 8, body8, 0)
    lax.fori_loop(0, (i % 8) // 4, body4, 0)
    lax.fori_loop(0, (i % 4) // 2, body2, 0)

    @pl.when(i % 2 == 0)
    def _():
        softmax_pv(i, sa_ref, True)

    @pl.when(i % 2 == 1)
    def _():
        scores(i, sb_ref)
        softmax_pv(i - 1, sa_ref, False)
        softmax_pv(i, sb_ref, True)

    o_t = jnp.concatenate(
        [jnp.concatenate([subs[h * halves + c][2][2][...] / subs[h * halves + c][2][1][...] for c in range(halves)], axis=1)
         for h in range(2)], axis=0)
    o_ref[...] = o_t.T.astype(o_ref.dtype)


def _moba(qa, ka, vat, kmean):
    bsz, seq, width = qa.shape
    nblk = seq // MOBA_BLOCK
    pairs = width // LANES
    t = MOBA_BLOCK
    return pl.pallas_call(
        _moba_kernel,
        grid=(bsz, pairs, nblk),
        in_specs=[
            pl.BlockSpec((None, t, LANES), lambda b, p, i: (b, i, p)),
            pl.BlockSpec((None, seq, LANES), lambda b, p, i: (b, 0, p)),
            pl.BlockSpec((None, nblk, LANES, t), lambda b, p, i: (b, 0, p, 0)),
            pl.BlockSpec((None, nblk, LANES), lambda b, p, i: (b, 0, p)),
        ],
        out_specs=pl.BlockSpec((None, t, LANES), lambda b, p, i: (b, i, p)),
        out_shape=jax.ShapeDtypeStruct((bsz, seq, width), bf16),
        scratch_shapes=[pltpu.VMEM((2, nblk, t), f32), pltpu.VMEM((2, t, LANES), bf16),
                        pltpu.VMEM((2, t, t), f32), pltpu.VMEM((2, t, t), f32)] + 2 * (t // LANES) * [
            pltpu.VMEM((1, LANES), f32), pltpu.VMEM((1, LANES), f32), pltpu.VMEM((HEAD_DIM, LANES), f32)],
        compiler_params=_params(("parallel", "parallel", "arbitrary")),
        name="moba_attention",
    )(qa, ka, vat, kmean)


DIL_TILE = 2048


def _dilated_kernel(q_ref, kp_ref, ko_ref, vp_ref, vo_ref, o_ref, kbuf, vbuf, og_ref, lg_ref):
    t = pl.program_id(1)
    tile = q_ref.shape[0]
    w = BAND_BLOCK
    kbuf[0:tile] = kp_ref[...]
    kbuf[tile:2 * tile] = ko_ref[...]
    vbuf[0:tile] = vp_ref[...]
    vbuf[tile:2 * tile] = vo_ref[...]
    qk_h0, v_h0 = _head_masks()
    qi = lax.broadcasted_iota(i32, (w, 2 * w), 0)
    kc = lax.broadcasted_iota(i32, (w, 2 * w), 1)

    def unit_group(g, dil, units):
        span = w * dil
        work = []
        for qs, first in units:
            if dil == 1:
                rows_q, rows_k = pl.ds(qs, w), pl.ds(tile - span + qs, 2 * w)
            else:
                rows_q, rows_k = pl.ds(qs, w, stride=dil), pl.ds(tile - span + qs, 2 * w, stride=dil)
            q = q_ref[rows_q, :]
            kk = kbuf[rows_k, :].astype(bf16)
            allowed = (kc >= jnp.maximum(qi, jnp.where(first, w, 0))) & (kc <= qi + w)
            scores = [_dot_nt(jnp.where(qk_h0 if h == 0 else jnp.logical_not(qk_h0), q, 0.0).astype(bf16), kk)
                      for h in range(2)]
            work.append((rows_q, rows_k, allowed, scores))
        soft = []
        for rows_q, rows_k, allowed, scores in work:
            per_head = []
            for s in scores:
                s = jnp.where(allowed, s, NEG)
                m = jnp.max(s, axis=-1, keepdims=True)
                e = jnp.exp(s - m)
                l = jnp.sum(e, axis=-1, keepdims=True)
                per_head.append((e.astype(bf16), l, m + jnp.log(l)))
            soft.append(per_head)
        for (rows_q, rows_k, _, _), per_head in zip(work, soft):
            vv = vbuf[rows_k, :].astype(bf16)
            outs = [_dot(e16, vv) / l for e16, l, _ in per_head]
            og_ref[g, rows_q, :] = jnp.where(v_h0, outs[0], outs[1])
            lg_ref[g, rows_q, :] = jnp.where(v_h0, per_head[0][2], per_head[1][2])

    first_tile = t == 0
    group = 4
    blocks = tile // w
    for g, (window, dil) in enumerate(DILATED_PAIRS):
        assert window // dil == w and tile % (w * dil) == 0 and blocks % group == 0

        def body(it, carry, g=g, dil=dil):
            units = []
            for uu in range(group):
                if dil == 1:
                    units.append((pl.multiple_of((group * it + uu) * w, w), first_tile & (group * it + uu == 0)))
                elif dil == group:
                    units.append((pl.multiple_of(it * w * dil, w * dil) + uu, first_tile & (it == 0)))
                else:
                    assert tile == w * dil
                    units.append((group * it + uu, first_tile))
            unit_group(g, dil, units)
            return carry

        lax.fori_loop(0, blocks // group, body, 0)

    l0, l1, l2 = lg_ref[0], lg_ref[1], lg_ref[2]
    mx = jnp.maximum(jnp.maximum(l0, l1), l2)
    e0, e1, e2 = jnp.exp(l0 - mx), jnp.exp(l1 - mx), jnp.exp(l2 - mx)
    den = e0 + e1 + e2
    out = (e0 / den) * og_ref[0] + (e1 / den) * og_ref[1] + (e2 / den) * og_ref[2]
    o_ref[...] = out.astype(o_ref.dtype)


def _dilated(qb, kb, vb):
    bsz, seq, width = qb.shape
    tile = DIL_TILE
    assert seq % tile == 0 and tile == BAND_BLOCK * max(d for _, d in DILATED_PAIRS)
    pairs = width // LANES
    own = lambda b, t, p: (b, t, p)
    prev = lambda b, t, p: (b, jnp.maximum(t - 1, 0), p)
    spec = lambda m: pl.BlockSpec((None, tile, LANES), m)
    return pl.pallas_call(
        _dilated_kernel,
        grid=(bsz, seq // tile, pairs),
        in_specs=[spec(own), spec(prev), spec(own), spec(prev), spec(own)],
        out_specs=spec(own),
        out_shape=jax.ShapeDtypeStruct((bsz, seq, width), bf16),
        scratch_shapes=[
            pltpu.VMEM((2 * tile, LANES), f32),
            pltpu.VMEM((2 * tile, LANES), f32),
            pltpu.VMEM((len(DILATED_PAIRS), tile, LANES), f32),
            pltpu.VMEM((len(DILATED_PAIRS), tile, LANES), f32),
        ],
        compiler_params=_params(("parallel", "parallel", "parallel")),
        name="dilated_attention",
    )(qb, kb, kb, vb, vb)


def _out_proj_ln_kernel(*refs, n_parts, alpha):
    o_refs = refs[:n_parts]
    w_refs = refs[n_parts:2 * n_parts]
    x_ref, g_ref, b_ref, y_ref = refs[2 * n_parts:]
    mix = _dot(o_refs[0][...], w_refs[0][...])
    for o_r, w_r in zip(o_refs[1:], w_refs[1:]):
        mix = mix + _dot(o_r[...], w_r[...])
    y_ref[...] = _layer_norm(alpha * x_ref[...] + mix, g_ref[...], b_ref[...])


def _out_proj_ln(parts, weights, x2d, g, b, alpha):
    n, d = x2d.shape
    tm = ROW_TILE
    row = lambda i: (i, 0)
    const = lambda i: (0, 0)
    in_specs = ([pl.BlockSpec((tm, p.shape[1]), row) for p in parts]
                + [pl.BlockSpec(w.shape, const) for w in weights]
                + [pl.BlockSpec((tm, d), row), pl.BlockSpec((1, d), const), pl.BlockSpec((1, d), const)])
    return pl.pallas_call(
        functools.partial(_out_proj_ln_kernel, n_parts=len(parts), alpha=alpha),
        grid=(n // tm,),
        in_specs=in_specs,
        out_specs=pl.BlockSpec((tm, d), row),
        out_shape=jax.ShapeDtypeStruct((n, d), f32),
        compiler_params=_params(("parallel",)),
        name="out_proj_ln",
    )(*parts, *weights, x2d, g.reshape(1, d), b.reshape(1, d))


def _router_kernel(x_ref, wh_ref, wl_ref, bias_ref, idx_ref, gate_ref, cnt_ref, run_ref):
    tm = x_ref.shape[0]
    ne = wh_ref.shape[0]
    eg = EXPERTS_PER_GROUP

    @pl.when(pl.program_id(0) == 0)
    def _():
        run_ref[...] = jnp.zeros(run_ref.shape, f32)

    x_s = _split(x_ref[...])
    wh, wl = wh_ref[...], wl_ref[...]
    logits = _dot_nt(wh, x_s[0]) + (_dot_nt(wh, x_s[1]) + _dot_nt(wl, x_s[0]))
    scores = _sigmoid(logits)
    sel = (scores + bias_ref[...]).reshape(N_GROUPS, eg, tm)
    loc = lax.broadcasted_iota(i32, (N_GROUPS, eg, tm), 1)
    m1 = jnp.max(sel, axis=1, keepdims=True)
    i1 = jnp.min(jnp.where(sel == m1, loc, eg), axis=1, keepdims=True)
    rest = jnp.where(loc == i1, -jnp.inf, sel)
    m2 = jnp.max(rest, axis=1, keepdims=True)
    i2 = jnp.min(jnp.where(rest == m2, loc, eg), axis=1, keepdims=True)
    gs = m1 + m2
    best, grp, l1, l2 = gs[0], jnp.zeros((1, tm), i32), i1[0], i2[0]
    for g in range(1, N_GROUPS):
        better = gs[g] > best
        best = jnp.where(better, gs[g], best)
        grp = jnp.where(better, g, grp)
        l1 = jnp.where(better, i1[g], l1)
        l2 = jnp.where(better, i2[g], l2)
    e1 = grp * eg + l1
    e2 = grp * eg + l2
    row = lax.broadcasted_iota(i32, (ne, tm), 0)
    oh1 = row == e1
    oh2 = row == e2
    g1 = jnp.sum(jnp.where(oh1, scores, 0.0), axis=0, keepdims=True)
    g2 = jnp.sum(jnp.where(oh2, scores, 0.0), axis=0, keepdims=True)
    gsum = g1 + g2
    both = jnp.where(oh1 | oh2, 1.0, 0.0)
    r_i = lax.broadcasted_iota(i32, (tm, tm), 0)
    c_i = lax.broadcasted_iota(i32, (tm, tm), 1)
    tri = jnp.where(r_i < c_i, 1.0, 0.0).astype(bf16)
    before = _dot(both.astype(bf16), tri) + run_ref[...]
    r1 = jnp.sum(jnp.where(oh1, before, 0.0), axis=0, keepdims=True)
    r2 = jnp.sum(jnp.where(oh2, before, 0.0), axis=0, keepdims=True)
    run_ref[...] = run_ref[...] + jnp.sum(both, axis=1, keepdims=True)
    pad = SUBLANES - 2 * TOP_K
    idx_ref[...] = jnp.concatenate([e1.astype(f32), e2.astype(f32), r1, r2] + pad * [r2], axis=0).astype(i32)
    gate_ref[...] = jnp.concatenate([g1 / gsum, g2 / gsum] + (SUBLANES - TOP_K) * [g2], axis=0)
    cnt_ref[...] = run_ref[...].astype(i32)


def _router(x2d, router_w, router_bias):
    n, d = x2d.shape
    tm = ROW_TILE
    ne = router_w.shape[1]
    wt = router_w.T
    wt_hi = wt.astype(bf16)
    wt_lo = (wt - wt_hi.astype(f32)).astype(bf16)
    return pl.pallas_call(
        _router_kernel,
        grid=(n // tm,),
        in_specs=[
            pl.BlockSpec((tm, d), lambda i: (i, 0)),
            pl.BlockSpec((ne, d), lambda i: (0, 0)),
            pl.BlockSpec((ne, d), lambda i: (0, 0)),
            pl.BlockSpec((ne, 1), lambda i: (0, 0)),
        ],
        out_specs=(
            pl.BlockSpec((SUBLANES, tm), lambda i: (0, i)),
            pl.BlockSpec((SUBLANES, tm), lambda i: (0, i)),
            pl.BlockSpec((ne, 1), lambda i: (0, 0)),
        ),
        out_shape=(
            jax.ShapeDtypeStruct((SUBLANES, n), i32),
            jax.ShapeDtypeStruct((SUBLANES, n), f32),
            jax.ShapeDtypeStruct((ne, 1), i32),
        ),
        scratch_shapes=[pltpu.VMEM((ne, 1), f32)],
        compiler_params=_params(("arbitrary",)),
        name="moe_router",
    )(x2d, wt_hi, wt_lo, router_bias.reshape(ne, 1))


def _row_dma_start(tm, copy):
    def start(g, c):
        base = pl.multiple_of(g * SUBLANES, SUBLANES)
        for r in range(SUBLANES):
            for k in range(TOP_K):
                copy(base + r, k).start()
        return c

    lax.fori_loop(0, tm // SUBLANES, start, 0)


def _row_dma_wait(tm, copy):
    def wait(g, c):
        base = pl.multiple_of(g * SUBLANES, SUBLANES)
        for r in range(SUBLANES):
            for k in range(TOP_K):
                copy(base + r, k).wait()
        return c

    lax.fori_loop(0, tm // SUBLANES, wait, 0)


def _row_dma_loops(tm, copy):
    _row_dma_start(tm, copy)
    _row_dma_wait(tm, copy)


def _to_tile_major(dst_ref, value):
    rows = value.shape[0]
    for c in range(value.shape[1] // LANES):
        dst_ref[pl.ds(c, rows, stride=SUBLANES), :] = value[:, c * LANES:(c + 1) * LANES]


def _from_tile_major(src_ref, rows, chunks):
    return jnp.concatenate([src_ref[pl.ds(c, rows, stride=SUBLANES), :] for c in range(chunks)], axis=1)


def _dispatch_kernel(d0_ref, d1_ref, x_ref, xs_in, xs_hbm, x3, sem):
    del xs_in
    tm = x_ref.shape[0]
    dests = (d0_ref, d1_ref)
    _to_tile_major(x3, x_ref[...])

    def copy(t, k):
        return pltpu.make_async_copy(x3.at[pl.ds(t * SUBLANES, SUBLANES)],
                                     xs_hbm.at[pl.ds(pl.multiple_of(dests[k][t], SUBLANES), SUBLANES)], sem)

    _row_dma_loops(tm, copy)


def _dispatch(x2d, dest8, cap):
    n, d = x2d.shape
    assert d == SUBLANES * LANES
    tm = ROW_TILE
    xs0 = jnp.zeros((cap * SUBLANES, LANES), x2d.dtype)
    return pl.pallas_call(
        _dispatch_kernel,
        grid=(n // tm,),
        in_specs=[
            pl.BlockSpec((tm,), lambda i: (i,), memory_space=pltpu.SMEM),
            pl.BlockSpec((tm,), lambda i: (i,), memory_space=pltpu.SMEM),
            pl.BlockSpec((tm, d), lambda i: (i, 0)),
            pl.BlockSpec(memory_space=pl.ANY),
        ],
        out_specs=pl.BlockSpec(memory_space=pl.ANY),
        out_shape=jax.ShapeDtypeStruct((cap * SUBLANES, LANES), x2d.dtype),
        scratch_shapes=[pltpu.VMEM((tm * SUBLANES, LANES), x2d.dtype), pltpu.SemaphoreType.DMA(())],
        input_output_aliases={3: 0},
        compiler_params=_params(("arbitrary",), has_side_effects=True, disable_bounds_checks=True),
        name="moe_dispatch",
    )(dest8[0], dest8[1], x2d, xs0)


def _expert_kernel(be_ref, nused_ref, xs_ref, w1_ref, w3_ref, w2_ref, ys_ref, w1b, w3b, w2b):
    b = pl.program_id(0)
    changed = jnp.logical_or(b == 0, be_ref[b] != be_ref[jnp.maximum(b - 1, 0)])

    @pl.when(changed)
    def _():
        w1b[...] = w1_ref[...].astype(bf16)
        w3b[...] = w3_ref[...].astype(bf16)
        w2b[...] = w2_ref[...].astype(bf16)

    @pl.when(b < nused_ref[0])
    def _():
        xb = _from_tile_major(xs_ref, MOE_BLOCK, w1b.shape[0] // LANES).astype(bf16)
        h1 = _dot(xb, w1b[...])
        h3 = _dot(xb, w3b[...])
        hb = (h1 * _sigmoid(h1)) * h3
        _to_tile_major(ys_ref, _dot(hb.astype(bf16), w2b[...]))

    @pl.when(b >= nused_ref[0])
    def _():
        ys_ref[...] = jnp.zeros(ys_ref.shape, f32)


def _expert_ffn(xs, block_expert, n_used, w1, w3, w2, layer):
    d, dff = w1.shape[-2:]
    cap = xs.shape[0] // SUBLANES
    nblocks = cap // MOE_BLOCK
    tile_rows = MOE_BLOCK * SUBLANES
    grid_spec = pltpu.PrefetchScalarGridSpec(
        num_scalar_prefetch=2,
        grid=(nblocks,),
        in_specs=[
            pl.BlockSpec((tile_rows, LANES), lambda b, be, nu: (b, 0)),
            pl.BlockSpec((None, None, d, dff), lambda b, be, nu: (layer, be[b], 0, 0)),
            pl.BlockSpec((None, None, d, dff), lambda b, be, nu: (layer, be[b], 0, 0)),
            pl.BlockSpec((None, None, dff, d), lambda b, be, nu: (layer, be[b], 0, 0)),
        ],
        out_specs=pl.BlockSpec((tile_rows, LANES), lambda b, be, nu: (b, 0)),
        scratch_shapes=[pltpu.VMEM((d, dff), bf16), pltpu.VMEM((d, dff), bf16), pltpu.VMEM((dff, d), bf16)],
    )
    return pl.pallas_call(
        _expert_kernel,
        grid_spec=grid_spec,
        out_shape=jax.ShapeDtypeStruct((cap * SUBLANES, LANES), f32),
        compiler_params=_params(("arbitrary",)),
        name="moe_expert_ffn",
    )(block_expert, n_used, xs, w1, w3, w2)


def _combine_kernel(d0_ref, d1_ref, n0_ref, n1_ref, x_ref, gate_ref, g_ref, b_ref, ys_hbm, y_ref, buf, sems, *, alpha):
    i = pl.program_id(0)
    last = pl.num_programs(0) - 1
    tm = x_ref.shape[0]

    def copier(dests, slot):
        def copy(t, k):
            return pltpu.make_async_copy(ys_hbm.at[pl.ds(pl.multiple_of(dests[k][t], SUBLANES), SUBLANES)],
                                         buf.at[slot, k, pl.ds(t * SUBLANES, SUBLANES)], sems.at[slot])
        return copy

    slot = i % 2

    @pl.when(i == 0)
    def _():
        _row_dma_start(tm, copier((d0_ref, d1_ref), 0))

    @pl.when(i < last)
    def _():
        _row_dma_start(tm, copier((n0_ref, n1_ref), 1 - slot))

    _row_dma_wait(tm, copier((d0_ref, d1_ref), slot))
    gates = gate_ref[...].T
    chunks = x_ref.shape[1] // LANES
    ffn = (_from_tile_major(buf.at[slot, 0], tm, chunks) * gates[:, 0:1]
           + _from_tile_major(buf.at[slot, 1], tm, chunks) * gates[:, 1:2])
    y_ref[...] = _layer_norm(alpha * x_ref[...] + ffn, g_ref[...], b_ref[...])


def _combine_ln(x2d, ys, dest, gates, g, b, alpha):
    n, d = x2d.shape
    tm = ROW_TILE
    tiles = n // tm
    cur = lambda i: (i,)
    nxt = lambda i: (jnp.minimum(i + 1, tiles - 1),)
    return pl.pallas_call(
        functools.partial(_combine_kernel, alpha=alpha),
        grid=(tiles,),
        in_specs=[
            pl.BlockSpec((tm,), cur, memory_space=pltpu.SMEM),
            pl.BlockSpec((tm,), cur, memory_space=pltpu.SMEM),
            pl.BlockSpec((tm,), nxt, memory_space=pltpu.SMEM),
            pl.BlockSpec((tm,), nxt, memory_space=pltpu.SMEM),
            pl.BlockSpec((tm, d), lambda i: (i, 0)),
            pl.BlockSpec((SUBLANES, tm), lambda i: (0, i)),
            pl.BlockSpec((1, d), lambda i: (0, 0)),
            pl.BlockSpec((1, d), lambda i: (0, 0)),
            pl.BlockSpec(memory_space=pl.ANY),
        ],
        out_specs=pl.BlockSpec((tm, d), lambda i: (i, 0)),
        out_shape=jax.ShapeDtypeStruct((n, d), f32),
        scratch_shapes=[pltpu.VMEM((2, TOP_K, tm * SUBLANES, LANES), f32), pltpu.SemaphoreType.DMA((2,))],
        compiler_params=_params(("arbitrary",), disable_bounds_checks=True),
        name="moe_combine_ln",
    )(dest[0], dest[1], dest[0], dest[1], x2d, gates, g.reshape(1, d), b.reshape(1, d), ys)


def _moe_ln(x2d, router_w, router_bias, w1, w3, w2, layer, g, b, alpha):
    n, d = x2d.shape
    idx, gates, counts = _router(x2d, router_w, router_bias)
    counts = counts.reshape(-1)
    padded = (counts + MOE_BLOCK - 1) // MOE_BLOCK * MOE_BLOCK
    pad_end = jnp.cumsum(padded)
    pad_start = pad_end - padded
    nblocks = n * TOP_K // MOE_BLOCK + N_EXPERTS
    cap = nblocks * MOE_BLOCK
    block_start = jnp.arange(nblocks, dtype=i32) * MOE_BLOCK
    block_expert = jnp.minimum(jnp.sum(pad_end[None, :] <= block_start[:, None], axis=1), N_EXPERTS - 1).astype(i32)
    n_used = (pad_end[-1:] // MOE_BLOCK).astype(i32)
    experts = jnp.arange(N_EXPERTS, dtype=i32)[:, None]
    dest = [((jnp.sum(jnp.where(idx[k][None, :] == experts, pad_start[:, None], 0), axis=0) + idx[TOP_K + k])
             * SUBLANES).astype(i32) for k in range(TOP_K)]
    xs = _dispatch(x2d, dest, cap)
    ys = _expert_ffn(xs, block_expert, n_used, w1, w3, w2, layer)
    return _combine_ln(x2d, ys, dest, gates, g, b, alpha)


def _gdn_proj_kernel(x_ref, wc_ref, wz_ref, wba_ref, cw_ref, alog_ref, dt_ref,
                     qkv_ref, z_ref, beta_ref, gc_ref, buf, *, tiles_per_seq):
    i = pl.program_id(0)
    tm = x_ref.shape[0]
    halo = 8
    xb = x_ref[...].astype(bf16)
    qk_cols = 2 * GDN_QK_HEADS * GDN_HEAD_DIM
    conv_ch = wc_ref.shape[1]

    @pl.when(i % tiles_per_seq == 0)
    def _():
        buf[0:halo, :] = jnp.zeros((halo, conv_ch), f32)

    buf[halo:halo + tm, :] = _dot(xb, wc_ref[...])
    q_scale = GDN_HEAD_DIM ** -0.5
    for c in range(conv_ch // LANES):
        cols = slice(c * LANES, (c + 1) * LANES)
        y = jnp.zeros((tm, LANES), f32)
        for j in range(GDN_CONV):
            shift = GDN_CONV - 1 - j
            y = y + buf[halo - shift:halo - shift + tm, cols] * cw_ref[j:j + 1, cols]
        y = y * _sigmoid(y)
        if c * LANES < qk_cols:
            y = y * lax.rsqrt(jnp.sum(y * y, axis=-1, keepdims=True) + NORM_EPS)
            if c * LANES < qk_cols // 2:
                y = y * q_scale
        qkv_ref[:, cols] = y
    buf[0:halo, :] = buf[tm:tm + halo, :]

    z_ref[...] = _dot(xb, wz_ref[...])
    ba = _dot_nt(wba_ref[...], xb)
    hv = GDN_V_HEADS
    beta_ref[...] = _sigmoid(ba[:hv])
    a = ba[hv:] + dt_ref[...]
    softplus = jnp.maximum(a, 0.0) + jnp.log(1.0 + jnp.exp(-jnp.abs(a)))
    g = -jnp.exp(alog_ref[...]) * softplus
    r_i = lax.broadcasted_iota(i32, (tm, tm), 0)
    c_i = lax.broadcasted_iota(i32, (tm, tm), 1)
    same_chunk = (r_i // GDN_CHUNK) == (c_i // GDN_CHUNK)
    cum = jnp.where(same_chunk & (r_i <= c_i), 1.0, 0.0)
    gc_ref[...] = _dot_hi(g, cum)


def _gdn_proj(x2d, w_conv, w_z, w_ba_t, conv_w, a_log, dt_bias, seq):
    n, d = x2d.shape
    tm = ROW_TILE // 2
    conv_ch = w_conv.shape[1]
    vdim = w_z.shape[1]
    hv = GDN_V_HEADS
    row = lambda i: (i, 0)
    const = lambda i: (0, 0)
    return pl.pallas_call(
        functools.partial(_gdn_proj_kernel, tiles_per_seq=seq // tm),
        grid=(n // tm,),
        in_specs=[
            pl.BlockSpec((tm, d), row),
            pl.BlockSpec(w_conv.shape, const, pipeline_mode=pl.Buffered(1)),
            pl.BlockSpec(w_z.shape, const, pipeline_mode=pl.Buffered(1)),
            pl.BlockSpec(w_ba_t.shape, const),
            pl.BlockSpec(conv_w.shape, const),
            pl.BlockSpec((hv, 1), const),
            pl.BlockSpec((hv, 1), const),
        ],
        out_specs=(
            pl.BlockSpec((tm, conv_ch), row),
            pl.BlockSpec((tm, vdim), row),
            pl.BlockSpec((hv, tm), lambda i: (0, i)),
            pl.BlockSpec((hv, tm), lambda i: (0, i)),
        ),
        out_shape=(
            jax.ShapeDtypeStruct((n, conv_ch), f32),
            jax.ShapeDtypeStruct((n, vdim), f32),
            jax.ShapeDtypeStruct((hv, n), f32),
            jax.ShapeDtypeStruct((hv, n), f32),
        ),
        scratch_shapes=[pltpu.VMEM((tm + 16, conv_ch), f32)],
        compiler_params=_params(("arbitrary",)),
        name="gdn_proj",
    )(x2d, w_conv, w_z, w_ba_t, conv_w, a_log.reshape(hv, 1), dt_bias.reshape(hv, 1))


def _gdn_delta_kernel(q_ref, k_ref, v_ref, z_ref, betar_ref, gcr_ref, ng_ref, o_ref, state):
    ts = q_ref.shape[0]
    c = GDN_CHUNK

    @pl.when(pl.program_id(2) == 0)
    def _():
        state[...] = jnp.zeros(state.shape, f32)

    hb = state.shape[0]
    hd = GDN_HEAD_DIM
    sub = GDN_SOLVE
    nch = sub // c
    r_i = lax.broadcasted_iota(i32, (sub, sub), 0)
    c_i = lax.broadcasted_iota(i32, (sub, sub), 1)
    same = (r_i // c) == (c_i // c)
    lower = same & (r_i >= c_i)
    strict = same & (r_i > c_i)
    inst = [(st, h) for st in range(ts // sub) for h in range(hb)]
    hs = range(len(inst))
    rows_of = [slice(st * sub, (st + 1) * sub) for st, _ in inst]
    q = [q_ref[rows_of[i], :] for i in hs]
    k = [k_ref[rows_of[i], :] for i in hs]
    k16 = [k[i].astype(bf16) for i in hs]
    qk = [_dot_nt(q[i].astype(bf16), k16[i]) for i in hs]

    rows8 = jnp.concatenate([betar_ref[h] for h in range(hb)] + [gcr_ref[h] for h in range(hb)]
                            + (SUBLANES - 2 * hb) * [gcr_ref[0]], axis=0)
    cols8 = rows8.T
    beta = [cols8[rows_of[i], h:h + 1] for i, (_, h) in enumerate(inst)]
    gcc = [cols8[rows_of[i], hb + h:hb + h + 1] for i, (_, h) in enumerate(inst)]
    gcr = [gcr_ref[h][:, rows_of[i]] for i, (_, h) in enumerate(inst)]
    decay = [jnp.where(lower, jnp.exp(jnp.where(lower, gcc[h] - gcr[h], 0.0)), 0.0) for h in hs]
    egc = [jnp.exp(gcc[h]) for h in hs]
    kb = [k[h] * beta[h] for h in hs]
    kk = [_dot_nt(kb[h].astype(bf16), k16[h]) for h in hs]
    l16 = [jnp.where(strict, kk[h] * decay[h], 0.0).astype(bf16) for h in hs]
    rhs = [jnp.concatenate([v_ref[rows_of[i], h * hd:(h + 1) * hd] * beta[i], kb[i] * egc[i]], axis=-1)
           for i, (_, h) in enumerate(inst)]
    pows = [l16]
    span = 2
    while span < c:
        pows.append([_dot(pows[-1][h], pows[-1][h]).astype(bf16) for h in hs])
        span *= 2
    sol = [rhs[h] - _dot(pows[0][h], rhs[h].astype(bf16)) for h in hs]
    for p in pows[1:]:
        sol = [sol[h] + _dot(p[h], sol[h].astype(bf16)) for h in hs]
    sol16 = [sol[h].astype(bf16) for h in hs]
    auw = [_dot((qk[h] * decay[h]).astype(bf16), sol16[h]) for h in hs]
    parts = []
    for i in hs:
        o0 = auw[i][:, :hd]
        qe16 = (q[i] * egc[i] - auw[i][:, hd:]).astype(bf16)
        ab, cd = [], []
        for n in range(nch):
            rows = slice(n * c, (n + 1) * c)
            g_last = gcr[i][:, (n + 1) * c - 1:(n + 1) * c]
            kd16 = (k[i][rows] * jnp.exp(g_last - gcc[i][rows])).astype(bf16)
            ab.append(_dot_tn(kd16, sol16[i][rows]))
            cd.append(jnp.exp(g_last))
        parts.append((o0, qe16, ab, cd))

    outs = [[] for _ in range(hb)]
    s_cur = [state[h] for h in range(hb)]
    for st in range(ts // sub):
        for n in range(nch):
            rows = slice(n * c, (n + 1) * c)
            for h in range(hb):
                o0, qe16, ab, cd = parts[st * hb + h]
                s16 = s_cur[h].astype(bf16)
                outs[h].append(_dot(qe16[rows], s16) + o0[rows])
                s_cur[h] = s_cur[h] * cd[n] - _dot(ab[n][:, hd:].astype(bf16), s16) + ab[n][:, :hd]
    for h in range(hb):
        state[h] = s_cur[h]
        o = jnp.concatenate(outs[h], axis=0)
        z = z_ref[:, h * hd:(h + 1) * hd]
        o = o * lax.rsqrt(jnp.mean(o * o, axis=-1, keepdims=True) + NORM_EPS) * ng_ref[...] * (z * _sigmoid(z))
        o_ref[:, h * hd:(h + 1) * hd] = o.astype(o_ref.dtype)


def _gdn_delta(qkv, z, beta_row, gc_row, norm_g, ts):
    bsz, seq, _ = qkv.shape
    hd = GDN_HEAD_DIM
    rep = GDN_V_HEADS // GDN_QK_HEADS
    tiles = seq // ts
    row_spec = pl.BlockSpec((rep, None, 1, ts), lambda b, g, s: (g, b * tiles + s, 0, 0))
    v_off = 2 * GDN_QK_HEADS // rep
    return pl.pallas_call(
        _gdn_delta_kernel,
        grid=(bsz, GDN_QK_HEADS, seq // ts),
        in_specs=[
            pl.BlockSpec((None, ts, hd), lambda b, g, s: (b, s, g)),
            pl.BlockSpec((None, ts, hd), lambda b, g, s: (b, s, GDN_QK_HEADS + g)),
            pl.BlockSpec((None, ts, rep * hd), lambda b, g, s: (b, s, v_off + g)),
            pl.BlockSpec((None, ts, rep * hd), lambda b, g, s: (b, s, g)),
            row_spec,
            row_spec,
            pl.BlockSpec((1, hd), lambda b, g, s: (0, 0)),
        ],
        out_specs=pl.BlockSpec((None, ts, rep * hd), lambda b, g, s: (b, s, g)),
        out_shape=jax.ShapeDtypeStruct((bsz, seq, GDN_V_HEADS * hd), bf16),
        scratch_shapes=[pltpu.VMEM((rep, hd, hd), f32)],
        compiler_params=_params(("parallel", "parallel", "arbitrary")),
        name="gdn_delta",
    )(qkv, qkv, qkv, z, beta_row, gc_row, norm_g.reshape(1, hd))


def _rope_slab_tables(seq):
    half = HEAD_DIM // 2
    inv = ROPE_THETA ** (-jnp.arange(0, HEAD_DIM, 2, dtype=f32) / HEAD_DIM)
    ang = jnp.arange(seq, dtype=f32)[:, None] * inv[None, :]
    cos, sin = jnp.cos(ang), jnp.sin(ang)
    cos_t = jnp.concatenate([cos] * (LANES // half), axis=1)
    sin_t = jnp.concatenate([-sin, -sin, sin, sin], axis=1)
    return cos_t, sin_t


def _attention_layer(x2d, bsz, seq, w_in, w_out, cos_t, sin_t, g, b, alpha):
    n, d = x2d.shape
    w_perm = w_in[:, _ab_col_perm()].astype(bf16)
    qa, ka, vat, qb, kb, vb, kmean = _ab_proj(x2d, w_perm, cos_t, sin_t, seq)
    sh = lambda t: t.reshape(bsz, seq, t.shape[-1])
    nblk = seq // MOBA_BLOCK
    o_a = _moba(sh(qa), sh(ka), vat.reshape(bsz, nblk, -1, MOBA_BLOCK), kmean.reshape(bsz, nblk, -1))
    o_b = _dilated(sh(qb), sh(kb), sh(vb))
    na = N_HEADS_A * HEAD_DIM
    w_o = w_out.astype(bf16)
    return _out_proj_ln([o_a.reshape(n, -1), o_b.reshape(n, -1)], [w_o[:na], w_o[na:]], x2d, g, b, alpha)


def _gdn_layer(x2d, bsz, seq, w_in, conv_w, a_log, dt_bias, norm_g, w_out, g, b, alpha):
    n, d = x2d.shape
    conv_ch = conv_w.shape[1]
    vdim = GDN_V_HEADS * GDN_HEAD_DIM
    w16 = w_in.astype(bf16)
    qkv, z, beta, gc = _gdn_proj(x2d, w16[:, :conv_ch], w16[:, conv_ch:conv_ch + vdim], w16[:, conv_ch + vdim:].T,
                                 conv_w, a_log, dt_bias, seq)
    rows = lambda t: t.reshape(GDN_V_HEADS, n // GDN_TILE, 1, GDN_TILE)
    o = _gdn_delta(qkv.reshape(bsz, seq, -1), z.reshape(bsz, seq, -1), rows(beta), rows(gc), norm_g, GDN_TILE)
    return _out_proj_ln([o.reshape(n, -1)], [w_out.astype(bf16)], x2d, g, b, alpha)


def kernel(x, ab_w_in, ab_w_out, gdn_w_in, gdn_conv_w, gdn_a_log, gdn_dt_bias, gdn_norm_g, gdn_w_out, mix_ln_g, mix_ln_b, router_w, router_bias, moe_w1, moe_w3, moe_w2, ffn_ln_g, ffn_ln_b):
    bsz, seq, d = x.shape
    depth = mix_ln_g.shape[0]
    alpha = (2.0 * depth) ** 0.25
    cos_t, sin_t = _rope_slab_tables(seq)
    h = x.reshape(bsz * seq, d)
    for layer in range(depth):
        j = layer // 2
        if layer % 2 == 0:
            h = _attention_layer(h, bsz, seq, ab_w_in[j], ab_w_out[j], cos_t, sin_t,
                                 mix_ln_g[layer], mix_ln_b[layer], alpha)
        else:
            h = _gdn_layer(h, bsz, seq, gdn_w_in[j], gdn_conv_w[j], gdn_a_log[j], gdn_dt_bias[j], gdn_norm_g[j],
                           gdn_w_out[j], mix_ln_g[layer], mix_ln_b[layer], alpha)
        h = _moe_ln(h, router_w, router_bias, moe_w1, moe_w3, moe_w2, layer, ffn_ln_g[layer], ffn_ln_b[layer], alpha)
    return h.reshape(bsz, seq, d)
```
